```python
import jax
import jax.numpy as jnp
from jax import lax
import numpy as np

D_MODEL = 1024
BATCH = 4
SEQ = 4096
DEPTH = 4

D_FF = 4 * D_MODEL
NORM_EPS = 1e-6
L2_EPS = 1e-6

RET_HEADS = 4
RET_DK = D_MODEL // 8
RET_DV = D_MODEL // 8
RET_CHUNK = 128
ROPE_BASE = 10000.0

SSD_HEADS = 8
SSD_HEAD_DIM = D_MODEL // 16
SSD_GROUPS = 2
SSD_STATE = 128
SSD_CONV = 4
SSD_CHUNK = 128
DT_MIN = 1e-3
DT_MAX = 1e-1

GDN_HEADS = 4
GDN_DK = D_MODEL // 8
GDN_DV = D_MODEL // 8
GDN_CONV = 4
GDN_CHUNK = 64

RET_QK = RET_HEADS * RET_DK
RET_V = RET_HEADS * RET_DV
SSD_D = SSD_HEADS * SSD_HEAD_DIM
SSD_BC = SSD_GROUPS * SSD_STATE
SSD_XBC = SSD_D + 2 * SSD_BC
GDN_QK = GDN_HEADS * GDN_DK
GDN_V = GDN_HEADS * GDN_DV
GDN_QKV = 2 * GDN_QK + GDN_V
MIX_WIDTH = RET_V + SSD_D + GDN_V
IN_SPLITS = (RET_QK, RET_QK, RET_V, RET_V, SSD_D, SSD_XBC, SSD_HEADS, GDN_QKV, GDN_V, GDN_HEADS, GDN_HEADS)
D_IN = sum(IN_SPLITS)

kernel_name = 'hybrid_retention_ssd_gdn_trunk'

F32 = jnp.float32


def _split_offsets():
    offs, acc = [], 0
    for size in IN_SPLITS[:-1]:
        acc += size
        offs.append(acc)
    return offs


def _rms(t, eps=NORM_EPS):
    t = t.astype(F32)
    return t * lax.rsqrt(jnp.mean(t * t, axis=-1, keepdims=True) + eps)


def _rms_norm(x, w):
    return (_rms(x) * w).astype(x.dtype)


def _l2norm(t):
    return t * lax.rsqrt(jnp.sum(t * t, axis=-1, keepdims=True) + L2_EPS)


def _causal_conv(t, w):
    k = w.shape[0]
    return lax.conv_general_dilated(
        t, w[:, None, :].astype(t.dtype), (1,), [(k - 1, 0)],
        dimension_numbers=('NWC', 'WIO', 'NWC'), feature_group_count=t.shape[-1])


def _rotary(t, positions):
    half = t.shape[-1] // 2
    inv_freq = ROPE_BASE ** (-jnp.arange(half, dtype=F32) / half)
    ang = positions.astype(F32)[:, :, None] * inv_freq
    cos = jnp.cos(ang)[:, :, None, :]
    sin = jnp.sin(ang)[:, :, None, :]
    t1, t2 = t[..., :half], t[..., half:]
    return jnp.concatenate([t1 * cos - t2 * sin, t2 * cos + t1 * sin], axis=-1)


def _to_chunks(t, c):
    b, s, h, d = t.shape
    return t.reshape(b, s // c, c, h, d).transpose(0, 3, 1, 2, 4)


def _from_chunks(t):
    b, h, n, c, d = t.shape
    return t.transpose(0, 2, 3, 1, 4).reshape(b, n * c, h, d)


def _prev_chunk_states(decay, inc):
    def step(state, xs):
        d, i = xs
        return d * state + i, state
    _, prev = lax.scan(step, jnp.zeros_like(inc[0]), (decay, inc))
    return prev


def _retention(q, k, v, positions):
    c = RET_CHUNK
    dk = q.shape[-1]
    q = _rotary(q.astype(F32), positions) * (dk ** -0.5)
    k = _rotary(k.astype(F32), positions)
    q, k, v = (_to_chunks(t, c) for t in (q, k, v.astype(F32)))
    h, n = q.shape[1], q.shape[2]
    log_gamma = jnp.log1p(-jnp.exp2(-5.0 - jnp.arange(h, dtype=F32)))
    idx = jnp.arange(c, dtype=F32)
    rel = idx[:, None] - idx[None, :]
    causal = rel >= 0
    d_intra = jnp.where(causal, jnp.exp(log_gamma[:, None, None] * jnp.where(causal, rel, 0.0)), 0.0)
    scores = jnp.einsum('bhncd,bhnmd->bhncm', q, k) * d_intra[None, :, None]
    y = jnp.einsum('bhncm,bhnme->bhnce', scores, v)
    zeta = jnp.exp(log_gamma[:, None] * (c - 1 - idx))
    kv = jnp.einsum('bhnmd,hm,bhnme->nbhde', k, zeta, v)
    chunk_decay = jnp.broadcast_to(jnp.exp(log_gamma * c)[None, None, :, None, None], (n, 1, h, 1, 1))
    prev = _prev_chunk_states(chunk_decay, kv)
    xi = jnp.exp(log_gamma[:, None] * (idx + 1))
    y = y + jnp.einsum('bhncd,nbhde,hc->bhnce', q, prev, xi)
    return _from_chunks(y)


def _retention_mixer(q, k, v, gate, positions, norm_w):
    b, s, _ = q.shape
    y = _retention(q.reshape(b, s, RET_HEADS, RET_DK), k.reshape(b, s, RET_HEADS, RET_DK),
                   v.reshape(b, s, RET_HEADS, RET_DV), positions)
    y = _rms(y) * norm_w.reshape(RET_HEADS, RET_DV)
    return y.reshape(b, s, RET_V) * jax.nn.silu(gate.astype(F32))


def _ssd_mixer(z, xbc, dt_raw, conv_w, conv_b, dt_bias, a_log, d_skip, norm_w):
    b, s, _ = xbc.shape
    l = SSD_CHUNK
    n = s // l
    g = SSD_GROUPS
    r = SSD_HEADS // g
    p = SSD_HEAD_DIM
    ks = SSD_STATE
    xbc = jax.nn.silu((_causal_conv(xbc, conv_w) + conv_b).astype(F32))
    xs, bm, cm = jnp.split(xbc, [SSD_D, SSD_D + SSD_BC], axis=-1)
    xs = xs.reshape(b, n, l, g, r, p)
    bm = bm.reshape(b, n, l, g, ks)
    cm = cm.reshape(b, n, l, g, ks)
    dt = jax.nn.softplus(dt_raw.astype(F32) + dt_bias).reshape(b, n, l, g, r)
    a = -jnp.exp(a_log.astype(F32)).reshape(g, r)
    a_cs = jnp.cumsum(dt * a, axis=2)
    causal = jnp.tril(jnp.ones((l, l), bool))[None, None, :, :, None, None]
    seg = a_cs[:, :, :, None] - a_cs[:, :, None, :]
    decay = jnp.where(causal, jnp.exp(jnp.where(causal, seg, 0.0)), 0.0)
    xdt = xs * dt[..., None]
    cb = jnp.einsum('bnlgk,bnmgk->bnlmg', cm, bm)
    y = jnp.einsum('bnlmg,bnlmgr,bnmgrp->bnlgrp', cb, decay, xdt)
    to_end = jnp.exp(a_cs[:, :, -1:] - a_cs)
    states = jnp.einsum('bnmgk,bnmgr,bnmgrp->nbgrpk', bm, to_end, xdt)
    chunk_decay = jnp.exp(a_cs[:, :, -1]).transpose(1, 0, 2, 3)[..., None, None]
    prev = _prev_chunk_states(chunk_decay, states)
    y = y + jnp.einsum('bnlgk,nbgrpk,bnlgr->bnlgrp', cm, prev, jnp.exp(a_cs))
    y = y + xs * d_skip.astype(F32).reshape(g, r)[:, :, None]
    gsz = SSD_D // g
    y = y.reshape(b, s, g, gsz) * jax.nn.silu(z.astype(F32)).reshape(b, s, g, gsz)
    return _rms(y).reshape(b, s, SSD_D) * norm_w


def _gated_delta_net(qkv, z, b_raw, a_raw, conv_w, dt_bias, a_log, norm_w):
    b, s, _ = qkv.shape
    h, dk, dv, c = GDN_HEADS, GDN_DK, GDN_DV, GDN_CHUNK
    qkv = jax.nn.silu(_causal_conv(qkv, conv_w).astype(F32))
    q, k, v = jnp.split(qkv, [GDN_QK, 2 * GDN_QK], axis=-1)
    q = _l2norm(q.reshape(b, s, h, dk)) * (dk ** -0.5)
    k = _l2norm(k.reshape(b, s, h, dk))
    v = v.reshape(b, s, h, dv)
    beta = jax.nn.sigmoid(b_raw.astype(F32))[..., None]
    g = -jnp.exp(a_log.astype(F32)) * jax.nn.softplus(a_raw.astype(F32) + dt_bias)
    q, k, v, kb, vb = (_to_chunks(t, c) for t in (q, k, v, k * beta, v * beta))
    g_cs = jnp.cumsum(_to_chunks(g[..., None], c)[..., 0], axis=-1)
    incl = jnp.tril(jnp.ones((c, c), bool))
    strict = jnp.tril(jnp.ones((c, c), bool), -1)
    diff = g_cs[..., :, None] - g_cs[..., None, :]
    decay = jnp.where(incl, jnp.exp(jnp.where(incl, diff, 0.0)), 0.0)
    lower = jnp.where(strict, jnp.einsum('bhncd,bhnmd->bhncm', kb, k) * decay, 0.0)
    rhs = jnp.concatenate([vb, kb * jnp.exp(g_cs)[..., None]], axis=-1)
    sol = lax.linalg.triangular_solve(lower + jnp.eye(c, dtype=F32), rhs,
                                      left_side=True, lower=True, unit_diagonal=True)
    u, w = sol[..., :dv], sol[..., dv:]
    attn = jnp.einsum('bhncd,bhnmd->bhncm', q, k) * decay
    g_last = g_cs[..., -1:]
    q_dec = q * jnp.exp(g_cs)[..., None]
    k_dec = k * jnp.exp(g_last - g_cs)[..., None]
    chunk_decay = jnp.exp(g_last)[..., None]

    def step(state, xs):
        qd, kd, u_n, w_n, a_n, dec = xs
        v_new = u_n - jnp.einsum('bhck,bhkv->bhcv', w_n, state)
        o = jnp.einsum('bhck,bhkv->bhcv', qd, state) + jnp.einsum('bhcm,bhmv->bhcv', a_n, v_new)
        state = state * dec + jnp.einsum('bhck,bhcv->bhkv', kd, v_new)
        return state, o

    xs = tuple(jnp.moveaxis(t, 2, 0) for t in (q_dec, k_dec, u, w, attn, chunk_decay))
    _, o = lax.scan(step, jnp.zeros((b, h, dk, dv), F32), xs)
    o = _from_chunks(jnp.moveaxis(o, 0, 2))
    o = _rms(o) * norm_w * jax.nn.silu(z.astype(F32)).reshape(b, s, h, dv)
    return o.reshape(b, s, GDN_V)


def setup_inputs(seed: int = 0) -> dict:
    key = jax.random.key(seed)
    ks = jax.random.split(key, 24)

    def nrm(k, shape, scale):
        return scale * jax.random.normal(k, shape, F32)

    def gain(k, shape):
        return 1.0 + 0.02 * jax.random.normal(k, shape, F32)

    def dt_bias(k, shape):
        dt = jnp.exp(jax.random.uniform(k, shape, F32, jnp.log(DT_MIN), jnp.log(DT_MAX)))
        return dt + jnp.log(-jnp.expm1(-dt))

    def a_log(k, shape):
        return jnp.log(jax.random.uniform(k, shape, F32, 1.0, 16.0))

    x = jax.random.normal(ks[0], (BATCH, SEQ, D_MODEL), F32)
    offset = jax.random.randint(ks[1], (BATCH, 1), 0, 2048, jnp.int32)
    positions = offset + jnp.arange(SEQ, dtype=jnp.int32)[None, :]
    return {
        'x': x,
        'positions': positions,
        'mix_norm_w': gain(ks[2], (DEPTH, D_MODEL)),
        'w_in': nrm(ks[3], (DEPTH, D_MODEL, D_IN), D_MODEL ** -0.5),
        'ret_norm_w': gain(ks[4], (DEPTH, RET_V)),
        'ssd_conv_w': nrm(ks[5], (DEPTH, SSD_CONV, SSD_XBC), SSD_CONV ** -0.5),
        'ssd_conv_b': nrm(ks[6], (DEPTH, SSD_XBC), 0.02),
        'ssd_dt_bias': dt_bias(ks[7], (DEPTH, SSD_HEADS)),
        'ssd_a_log': a_log(ks[8], (DEPTH, SSD_HEADS)),
        'ssd_d': 1.0 + nrm(ks[9], (DEPTH, SSD_HEADS), 0.1),
        'ssd_norm_w': gain(ks[10], (DEPTH, SSD_D)),
        'gdn_conv_w': nrm(ks[11], (DEPTH, GDN_CONV, GDN_QKV), GDN_CONV ** -0.5),
        'gdn_dt_bias': dt_bias(ks[12], (DEPTH, GDN_HEADS)),
        'gdn_a_log': a_log(ks[13], (DEPTH, GDN_HEADS)),
        'gdn_norm_w': gain(ks[14], (DEPTH, GDN_DV)),
        'w_out': nrm(ks[15], (DEPTH, MIX_WIDTH, D_MODEL), MIX_WIDTH ** -0.5),
        'mlp_norm_w': gain(ks[16], (DEPTH, D_MODEL)),
        'w_up': nrm(ks[17], (DEPTH, D_MODEL, D_FF), D_MODEL ** -0.5),
        'w_down': nrm(ks[18], (DEPTH, D_FF, D_MODEL), D_FF ** -0.5),
        'final_norm_w': gain(ks[19], (D_MODEL,)),
    }


def reference(x, positions, mix_norm_w, w_in, ret_norm_w, ssd_conv_w, ssd_conv_b, ssd_dt_bias,
              ssd_a_log, ssd_d, ssd_norm_w, gdn_conv_w, gdn_dt_bias, gdn_a_log, gdn_norm_w,
              w_out, mlp_norm_w, w_up, w_down, final_norm_w):
    offsets = _split_offsets()
    for l in range(DEPTH):
        h = _rms_norm(x, mix_norm_w[l])
        rq, rk, rv, rg, sz, sxbc, sdt, gqkv, gz, gb, ga = jnp.split(h @ w_in[l], offsets, axis=-1)
        y_ret = _retention_mixer(rq, rk, rv, rg, positions, ret_norm_w[l])
        y_ssd = _ssd_mixer(sz, sxbc, sdt, ssd_conv_w[l], ssd_conv_b[l], ssd_dt_bias[l],
                           ssd_a_log[l], ssd_d[l], ssd_norm_w[l])
        y_gdn = _gated_delta_net(gqkv, gz, gb, ga, gdn_conv_w[l], gdn_dt_bias[l],
                                 gdn_a_log[l], gdn_norm_w[l])
        y = jnp.concatenate([y_ret, y_ssd, y_gdn], axis=-1).astype(x.dtype)
        x = x + y @ w_out[l]
        hm = _rms_norm(x, mlp_norm_w[l]) @ w_up[l]
        x = x + jnp.square(jax.nn.relu(hm)) @ w_down[l]
    return _rms_norm(x, final_norm_w)
```

```python
import functools
import math

import numpy as np
import jax
import jax.numpy as jnp
from jax import lax
from jax.experimental import pallas as pl
from jax.experimental.pallas import tpu as pltpu

F32 = jnp.float32
BF16 = jnp.bfloat16
HI = lax.Precision.HIGHEST

NORM_EPS = 1e-6
L2_EPS = 1e-6
ROPE_BASE = 10000.0

RET_HEADS = 4
RET_CHUNK = 128
SSD_HEADS = 8
SSD_GROUPS = 2
SSD_CHUNK = 128
GDN_HEADS = 4
GDN_CHUNK = 64
CONV_K = 4

LANE = 128
SEG = 512
CONV_PAD = 8
VMEM_LIMIT = 56 * 1024 * 1024

SEG_RQ, SEG_RK, SEG_RV, SEG_RG, SEG_SZ, SEG_SX, SEG_SBC, SEG_GQ, SEG_GK, SEG_GV, SEG_GZ = range(11)
N_SEG = 11
SMALL_COL = N_SEG * SEG
D_PROJ = SMALL_COL + LANE
GDN_BETA_LANE = SSD_HEADS
GDN_DECAY_LANE = SSD_HEADS + GDN_HEADS


def _dot(a, b, precision=None):
    return jnp.dot(a, b, preferred_element_type=F32, precision=precision)


def _dot_nt(a, b, precision=None):
    return lax.dot_general(a, b, (((1,), (1,)), ((), ())), preferred_element_type=F32, precision=precision)


def _dot_tn(a, b, precision=None):
    return lax.dot_general(a, b, (((0,), (0,)), ((), ())), preferred_element_type=F32, precision=precision)


def _silu(t):
    return t * jax.nn.sigmoid(t)


def _softplus(t):
    return jnp.maximum(t, 0.0) + jnp.log1p(jnp.exp(-jnp.abs(t)))


def _rms_rows(t, eps=NORM_EPS):
    return t * lax.rsqrt(jnp.mean(t * t, axis=-1, keepdims=True) + eps)


def _params(*sem):
    return pltpu.CompilerParams(dimension_semantics=sem, vmem_limit_bytes=VMEM_LIMIT)


def _rope_kernel(pos_ref, freq_ref, cos_ref, sin_ref):
    ang = pos_ref[...].astype(F32) * freq_ref[...]
    lane = lax.broadcasted_iota(jnp.int32, ang.shape, 1)
    cos_ref[...] = jnp.cos(ang)
    sin_ref[...] = jnp.where(lane < LANE // 2, -jnp.sin(ang), jnp.sin(ang))


def _rope_tables(positions, tb=1024):
    t = positions.size
    half = LANE // 2
    inv_freq = ROPE_BASE ** (-jnp.arange(half, dtype=F32) / half)
    freq = jnp.concatenate([inv_freq, inv_freq])[None, :]
    tb = min(tb, t)
    return pl.pallas_call(
        _rope_kernel,
        grid=(t // tb,),
        in_specs=[pl.BlockSpec((tb, 1), lambda i: (i, 0)), pl.BlockSpec((1, LANE), lambda i: (0, 0))],
        out_specs=[pl.BlockSpec((tb, LANE), lambda i: (i, 0))] * 2,
        out_shape=[jax.ShapeDtypeStruct((t, LANE), F32)] * 2,
        compiler_params=_params("parallel"),
        name="rope_tables",
    )(positions.reshape(t, 1), freq)


def _inproj_kernel(x_ref, nw_ref, w_ref, o_ref, xn_ref):
    @pl.when(pl.program_id(1) == 0)
    def _():
        xn_ref[...] = (_rms_rows(x_ref[...]) * nw_ref[...]).astype(BF16)

    o_ref[...] = _dot(xn_ref[...], w_ref[...])


def _in_proj(x, norm_w, w, tm=1024, tn=1152):
    t, d = x.shape
    n = w.shape[1]
    tm = min(tm, t)
    return pl.pallas_call(
        _inproj_kernel,
        grid=(t // tm, n // tn),
        in_specs=[pl.BlockSpec((tm, d), lambda i, j: (i, 0)),
                  pl.BlockSpec((1, d), lambda i, j: (0, 0)),
                  pl.BlockSpec((d, tn), lambda i, j: (0, j))],
        out_specs=pl.BlockSpec((tm, tn), lambda i, j: (i, j)),
        out_shape=jax.ShapeDtypeStruct((t, n), F32),
        scratch_shapes=[pltpu.VMEM((tm, d), BF16)],
        compiler_params=_params("parallel", "arbitrary"),
        name="in_proj",
    )(x, norm_w, w)


def _ret_consts():
    c = RET_CHUNK
    h = RET_HEADS
    log_gamma = jnp.log1p(-jnp.exp2(-5.0 - jnp.arange(h, dtype=F32)))
    idx = jnp.arange(c, dtype=F32)
    rel = idx[:, None] - idx[None, :]
    causal = rel >= 0
    d_intra = jnp.where(causal, jnp.exp(log_gamma[:, None, None] * jnp.where(causal, rel, 0.0)), 0.0)
    zeta = jnp.exp(log_gamma[:, None] * (c - 1 - idx))
    xi = jnp.exp(log_gamma[:, None] * (idx + 1))
    cdec = jnp.exp(log_gamma * c)
    ones = jnp.ones((h, c, LANE), F32)
    tab = jnp.stack([zeta[:, :, None] * ones, xi[:, :, None] * ones, cdec[:, None, None] * ones], axis=1)
    return d_intra, tab


def _ret_kernel(q_ref, k_ref, v_ref, g_ref, cos_ref, sin_ref, dm_ref, tab_ref, nw_ref, o_ref, st_ref, *, nchunk):
    c = RET_CHUNK

    @pl.when(pl.program_id(1) == 0)
    def _():
        st_ref[...] = jnp.zeros_like(st_ref)

    scale = LANE ** -0.5
    for ci in range(nchunk):
        rows = pl.ds(ci * c, c)
        cos = cos_ref[rows, :]
        sin = sin_ref[rows, :]
        for h in range(RET_HEADS):
            cols = pl.ds(h * LANE, LANE)
            q = q_ref[rows, cols]
            k = k_ref[rows, cols]
            v = v_ref[rows, cols]
            q = (q * cos + pltpu.roll(q, LANE // 2, 1) * sin) * scale
            k = k * cos + pltpu.roll(k, LANE // 2, 1) * sin
            zeta = tab_ref[h, 0]
            xi = tab_ref[h, 1]
            cdec = tab_ref[h, 2]
            qb = q.astype(BF16)
            kb = k.astype(BF16)
            vb = v.astype(BF16)
            scores = _dot_nt(qb, kb) * dm_ref[h]
            st = st_ref[h]
            y = _dot(scores.astype(BF16), vb) + _dot(qb, st.astype(BF16)) * xi
            st_ref[h] = cdec * st + _dot_tn((k * zeta).astype(BF16), vb)
            y = _rms_rows(y) * nw_ref[:, cols] * _silu(g_ref[rows, cols])
            o_ref[rows, cols] = y


def _retention(proj, cosf, sinf, norm_w, batch, tb=512):
    t = proj.shape[0]
    s = t // batch
    tb = min(tb, s)
    nb = s // tb
    d_intra, tab = _ret_consts()
    seg = lambda j: pl.BlockSpec((tb, SEG), lambda b, i, j=j: (b * nb + i, j))
    row = pl.BlockSpec((tb, LANE), lambda b, i: (b * nb + i, 0))
    full = lambda a: pl.BlockSpec(a.shape, lambda b, i: (0,) * a.ndim)
    return pl.pallas_call(
        functools.partial(_ret_kernel, nchunk=tb // RET_CHUNK),
        grid=(batch, nb),
        in_specs=[seg(SEG_RQ), seg(SEG_RK), seg(SEG_RV), seg(SEG_RG), row, row,
                  full(d_intra), full(tab), full(norm_w)],
        out_specs=pl.BlockSpec((tb, SEG), lambda b, i: (b * nb + i, 0)),
        out_shape=jax.ShapeDtypeStruct((t, SEG), F32),
        scratch_shapes=[pltpu.VMEM((RET_HEADS, LANE, LANE), F32)],
        compiler_params=_params("parallel", "arbitrary"),
        name="retention",
    )(proj, proj, proj, proj, cosf, sinf, d_intra, tab, norm_w)


def _causal_conv(x_ref, ext_ref, w_ref, first):
    tb = x_ref.shape[0]

    @pl.when(first)
    def _():
        ext_ref[pl.ds(0, CONV_PAD), :] = jnp.zeros((CONV_PAD, ext_ref.shape[1]), F32)

    x = x_ref[...]
    ext_ref[pl.ds(CONV_PAD, tb), :] = x
    acc = x * w_ref[CONV_K - 1:CONV_K, :]
    for j in range(CONV_K - 1):
        shift = CONV_K - 1 - j
        acc = acc + ext_ref[pl.ds(CONV_PAD - shift, tb), :] * w_ref[j:j + 1, :]
    ext_ref[pl.ds(0, CONV_PAD), :] = x[tb - CONV_PAD:, :]
    return acc


def _tri(n, strict=False, upper=False):
    r = lax.broadcasted_iota(jnp.int32, (n, n), 0)
    c = lax.broadcasted_iota(jnp.int32, (n, n), 1)
    if upper:
        r, c = c, r
    return (r > c) if strict else (r >= c)


def _ssd_kernel(z_ref, x_ref, bc_ref, sm_ref, cwx_ref, cwbc_ref, cbx_ref, cbbc_ref, hp_ref, e_ref, nw_ref,
                o_ref, extx_ref, extbc_ref, st_ref, *, nchunk):
    c = SSD_CHUNK
    nh = SSD_HEADS
    first = pl.program_id(1) == 0

    @pl.when(first)
    def _():
        st_ref[...] = jnp.zeros_like(st_ref)

    xs_all = _silu(_causal_conv(x_ref, extx_ref, cwx_ref, first) + cbx_ref[...])
    bc_all = _silu(_causal_conv(bc_ref, extbc_ref, cwbc_ref, first) + cbbc_ref[...])
    dt_bias = hp_ref[0:1, :]
    a_neg = -jnp.exp(hp_ref[1:2, :])
    d_skip = hp_ref[2:3, :]
    expand = e_ref[...]
    d_full = _dot(d_skip, expand, HI)
    causal = _tri(c)
    causal_f = causal.astype(F32)
    upper_f = _tri(c, upper=True).astype(F32)
    lane = lax.broadcasted_iota(jnp.int32, (c, LANE), 1)
    gw = SEG // SSD_GROUPS
    ks = LANE

    for ci in range(nchunk):
        r0 = ci * c
        xs = xs_all[r0:r0 + c, :]
        bc = bc_all[r0:r0 + c, :]
        dt = _softplus(sm_ref[pl.ds(r0, c), :] + dt_bias)
        da = dt * a_neg
        acs_col = _dot(causal_f, da, HI)
        acs_row = _dot_tn(da, upper_f, HI)
        a_last = acs_col[c - 1:c, :]
        dt_full = _dot(dt, expand, HI)
        ea_full = _dot(jnp.exp(acs_col), expand, HI)
        te_full = _dot(jnp.exp(a_last - acs_col), expand, HI)
        cd_full = _dot(jnp.exp(a_last), expand, HI)
        xdt = xs * dt_full
        y_parts = []
        for g in range(SSD_GROUPS):
            bm = bc[:, g * ks:(g + 1) * ks].astype(BF16)
            cm = bc[:, SSD_GROUPS * ks + g * ks:SSD_GROUPS * ks + (g + 1) * ks].astype(BF16)
            cb = _dot_nt(cm, bm)
            gcols = slice(g * gw, (g + 1) * gw)
            prev = st_ref[g]
            y_inter = _dot(cm, prev.astype(BF16)) * ea_full[:, gcols]
            st_ref[g] = cd_full[:, gcols] * prev + _dot_tn(bm, (xdt[:, gcols] * te_full[:, gcols]).astype(BF16))
            for pair in range(gw // LANE):
                ms = []
                for sub in range(2):
                    hd = g * (nh // SSD_GROUPS) + pair * 2 + sub
                    seg = acs_col[:, hd:hd + 1] - acs_row[hd:hd + 1, :]
                    dec = jnp.where(causal, jnp.exp(jnp.where(causal, seg, 0.0)), 0.0)
                    ms.append((cb * dec).astype(BF16))
                xp = xdt[:, g * gw + pair * LANE:g * gw + (pair + 1) * LANE]
                lo = jnp.where(lane < LANE // 2, xp, 0.0).astype(BF16)
                hi = jnp.where(lane >= LANE // 2, xp, 0.0).astype(BF16)
                y_pair = _dot(jnp.concatenate(ms, axis=1), jnp.concatenate([lo, hi], axis=0))
                y_parts.append(y_pair + y_inter[:, pair * LANE:(pair + 1) * LANE])
        y = jnp.concatenate(y_parts, axis=1) + xs * d_full
        y = y * _silu(z_ref[pl.ds(r0, c), :])
        outs = [_rms_rows(y[:, g * gw:(g + 1) * gw]) for g in range(SSD_GROUPS)]
        o_ref[pl.ds(r0, c), :] = jnp.concatenate(outs, axis=1) * nw_ref[...]


def _pad_rows(a, rows=8, cols=LANE):
    out = jnp.zeros((rows, cols), F32)
    return out.at[:a.shape[0], :a.shape[1]].set(a.astype(F32))


def _ssd(proj, conv_w, conv_b, dt_bias, a_log, d_skip, norm_w, batch, tb=512):
    t = proj.shape[0]
    s = t // batch
    tb = min(tb, s)
    nb = s // tb
    hp = _pad_rows(jnp.stack([dt_bias, a_log, d_skip]))
    expand = jnp.repeat(jnp.eye(LANE, SSD_HEADS, dtype=F32), SEG // SSD_HEADS, axis=1)
    cwx, cwbc = conv_w[:, :SEG], conv_w[:, SEG:]
    cbx, cbbc = conv_b[None, :SEG], conv_b[None, SEG:]
    seg = lambda j: pl.BlockSpec((tb, SEG), lambda b, i, j=j: (b * nb + i, j))
    small = pl.BlockSpec((tb, LANE), lambda b, i: (b * nb + i, SMALL_COL // LANE))
    full = lambda a: pl.BlockSpec(a.shape, lambda b, i: (0,) * a.ndim)
    consts = [cwx, cwbc, cbx, cbbc, hp, expand, norm_w]
    return pl.pallas_call(
        functools.partial(_ssd_kernel, nchunk=tb // SSD_CHUNK),
        grid=(batch, nb),
        in_specs=[seg(SEG_SZ), seg(SEG_SX), seg(SEG_SBC), small] + [full(a) for a in consts],
        out_specs=pl.BlockSpec((tb, SEG), lambda b, i: (b * nb + i, 0)),
        out_shape=jax.ShapeDtypeStruct((t, SEG), F32),
        scratch_shapes=[pltpu.VMEM((tb + CONV_PAD, SEG), F32), pltpu.VMEM((tb + CONV_PAD, SEG), F32),
                        pltpu.VMEM((SSD_GROUPS, LANE, SEG // SSD_GROUPS), F32)],
        compiler_params=_params("parallel", "arbitrary"),
        name="ssd",
    )(proj, proj, proj, proj, *consts)


def _gdn_kernel(q_ref, k_ref, v_ref, z_ref, sm_ref, cwq_ref, cwk_ref, cwv_ref, hp_ref, nw_ref,
                o_ref, extq_ref, extk_ref, extv_ref, st_ref, *, nchunk):
    c = GDN_CHUNK
    nh = GDN_HEADS
    first = pl.program_id(1) == 0

    @pl.when(first)
    def _():
        st_ref[...] = jnp.zeros_like(st_ref)

    q_all = _silu(_causal_conv(q_ref, extq_ref, cwq_ref, first))
    k_all = _silu(_causal_conv(k_ref, extk_ref, cwk_ref, first))
    v_all = _silu(_causal_conv(v_ref, extv_ref, cwv_ref, first))
    dt_bias = hp_ref[0:1, :]
    a_neg = -jnp.exp(hp_ref[1:2, :])
    incl = _tri(c)
    strict = _tri(c, strict=True)
    incl_f = incl.astype(F32)
    upper_f = _tri(c, upper=True).astype(F32)
    scale = LANE ** -0.5

    for ci in range(nchunk):
        r0 = ci * c
        rows = pl.ds(r0, c)
        sm = sm_ref[rows, :]
        beta = jax.nn.sigmoid(sm)
        g = a_neg * _softplus(sm + dt_bias)
        gcs_col = _dot(incl_f, g, HI)
        gcs_row = _dot_tn(g, upper_f, HI)
        g_last = gcs_col[c - 1:c, :]
        eg_all = jnp.exp(gcs_col)
        ekd_all = jnp.exp(g_last - gcs_col)
        cdec_all = jnp.exp(g_last)
        for h in range(nh):
            cols = slice(h * LANE, (h + 1) * LANE)
            dl = GDN_DECAY_LANE + h
            eg = eg_all[:, dl:dl + 1]
            ekd = ekd_all[:, dl:dl + 1]
            cdec = cdec_all[:, dl:dl + 1]
            q = q_all[r0:r0 + c, cols]
            k = k_all[r0:r0 + c, cols]
            v = v_all[r0:r0 + c, cols]
            q = q * lax.rsqrt(jnp.sum(q * q, axis=-1, keepdims=True) + L2_EPS) * scale
            k = k * lax.rsqrt(jnp.sum(k * k, axis=-1, keepdims=True) + L2_EPS)
            bh = beta[:, GDN_BETA_LANE + h:GDN_BETA_LANE + h + 1]
            kb = k * bh
            vb = v * bh
            diff = gcs_col[:, dl:dl + 1] - gcs_row[dl:dl + 1, :]
            decay = jnp.where(incl, jnp.exp(jnp.where(incl, diff, 0.0)), 0.0)
            kbf = k.astype(BF16)
            lower = jnp.where(strict, _dot_nt(kb.astype(BF16), kbf) * decay, 0.0)
            attn = _dot_nt(q.astype(BF16), kbf) * decay
            rhs = jnp.concatenate([vb, kb * eg], axis=1)
            p = -lower
            sol = rhs + _dot(p, rhs, HI)
            span = 2
            while span < c:
                p = _dot(p, p, HI)
                sol = sol + _dot(p, sol, HI)
                span *= 2
            u = sol[:, :LANE]
            w = sol[:, LANE:]
            st = st_ref[h]
            stb = st.astype(BF16)
            v_new = u - _dot(w.astype(BF16), stb)
            o = _dot((q * eg).astype(BF16), stb) + _dot(attn.astype(BF16), v_new.astype(BF16))
            st_ref[h] = st * cdec + _dot_tn((k * ekd).astype(BF16), v_new.astype(BF16))
            o = _rms_rows(o) * nw_ref[...] * _silu(z_ref[rows, cols])
            o_ref[rows, cols] = o


def _gdn(proj, conv_w, dt_bias, a_log, norm_w, batch, tb=256):
    t = proj.shape[0]
    s = t // batch
    tb = min(tb, s)
    nb = s // tb
    hp = jnp.zeros((8, LANE), F32).at[0:2, GDN_DECAY_LANE:GDN_DECAY_LANE + GDN_HEADS].set(
        jnp.stack([dt_bias, a_log]).astype(F32))
    cws =[conv_w[:, i * SEG:(i + 1) * SEG] for i in range(3)]
    seg = lambda j: pl.BlockSpec((tb, SEG), lambda b, i, j=j: (b * nb + i, j))
    small = pl.BlockSpec((tb, LANE), lambda b, i: (b * nb + i, SMALL_COL // LANE))
    full = lambda a: pl.BlockSpec(a.shape, lambda b, i: (0,) * a.ndim)
    consts = cws + [hp, norm_w[None, :]]
    return pl.pallas_call(
        functools.partial(_gdn_kernel, nchunk=tb // GDN_CHUNK),
        grid=(batch, nb),
        in_specs=[seg(SEG_GQ), seg(SEG_GK), seg(SEG_GV), seg(SEG_GZ), small] + [full(a) for a in consts],
        out_specs=pl.BlockSpec((tb, SEG), lambda b, i: (b * nb + i, 0)),
        out_shape=jax.ShapeDtypeStruct((t, SEG), F32),
        scratch_shapes=[pltpu.VMEM((tb + CONV_PAD, SEG), F32)] * 3 + [pltpu.VMEM((GDN_HEADS, LANE, LANE), F32)],
        compiler_params=_params("parallel", "arbitrary"),
        name="gdn",
    )(proj, proj, proj, proj, proj, *consts)


def _outproj_kernel(x_ref, y0_ref, y1_ref, y2_ref, w_ref, o_ref):
    acc = x_ref[...]
    for i, y_ref in enumerate((y0_ref, y1_ref, y2_ref)):
        acc = acc + _dot(y_ref[...].astype(BF16), w_ref[pl.ds(i * SEG, SEG), :])
    o_ref[...] = acc


def _out_proj(x, ys, w, tm=512):
    t, d = x.shape
    tm = min(tm, t)
    return pl.pallas_call(
        _outproj_kernel,
        grid=(t // tm,),
        in_specs=[pl.BlockSpec((tm, d), lambda i: (i, 0))] + [pl.BlockSpec((tm, SEG), lambda i: (i, 0))] * 3
                 + [pl.BlockSpec(w.shape, lambda i: (0, 0))],
        out_specs=pl.BlockSpec((tm, d), lambda i: (i, 0)),
        out_shape=jax.ShapeDtypeStruct((t, d), F32),
        compiler_params=_params("parallel"),
        name="out_proj",
    )(x, *ys, w)


def _mlp_kernel(x_ref, nw_ref, wu_ref, wd_ref, fw_ref, o_ref, xn_ref, acc_ref, *, final_norm):
    j = pl.program_id(1)

    @pl.when(j == 0)
    def _():
        xn_ref[...] = (_rms_rows(x_ref[...]) * nw_ref[...]).astype(BF16)
        acc_ref[...] = jnp.zeros_like(acc_ref)

    h = jnp.maximum(_dot(xn_ref[...], wu_ref[...]), 0.0)
    acc_ref[...] += _dot((h * h).astype(BF16), wd_ref[...])

    @pl.when(j == pl.num_programs(1) - 1)
    def _():
        out = x_ref[...] + acc_ref[...]
        if final_norm:
            out = _rms_rows(out) * fw_ref[...]
        o_ref[...] = out


def _mlp(x, norm_w, w_up, w_down, final_w, final_norm, tm=1024, tf=1024):
    t, d = x.shape
    ff = w_up.shape[1]
    tm = min(tm, t)
    return pl.pallas_call(
        functools.partial(_mlp_kernel, final_norm=final_norm),
        grid=(t // tm, ff // tf),
        in_specs=[pl.BlockSpec((tm, d), lambda i, j: (i, 0)),
                  pl.BlockSpec((1, d), lambda i, j: (0, 0)),
                  pl.BlockSpec((d, tf), lambda i, j: (0, j)),
                  pl.BlockSpec((tf, d), lambda i, j: (j, 0)),
                  pl.BlockSpec((1, d), lambda i, j: (0, 0))],
        out_specs=pl.BlockSpec((tm, d), lambda i, j: (i, 0)),
        out_shape=jax.ShapeDtypeStruct((t, d), F32),
        scratch_shapes=[pltpu.VMEM((tm, d), BF16), pltpu.VMEM((tm, d), F32)],
        compiler_params=_params("parallel", "arbitrary"),
        name="mlp",
    )(x, norm_w, w_up, w_down, final_w)


def _reorder_w_in(w):
    sizes = (SEG, SEG, SEG, SEG, SEG, 2 * SEG, SSD_HEADS, 3 * SEG, SEG, GDN_HEADS, GDN_HEADS)
    offs = np.concatenate([[0], np.cumsum(sizes)])
    piece = lambda i: w[:, offs[i]:offs[i + 1]]
    small = jnp.concatenate([piece(6), piece(9), piece(10)], axis=1)
    small = jnp.pad(small, ((0, 0), (0, LANE - small.shape[1])))
    big = [piece(i) for i in (0, 1, 2, 3, 4, 5, 7, 8)]
    return jnp.concatenate(big + [small], axis=1).astype(BF16)


def kernel(x, positions, mix_norm_w, w_in, ret_norm_w, ssd_conv_w, ssd_conv_b, ssd_dt_bias, ssd_a_log, ssd_d,
           ssd_norm_w, gdn_conv_w, gdn_dt_bias, gdn_a_log, gdn_norm_w, w_out, mlp_norm_w, w_up, w_down,
           final_norm_w):
    batch, s, d = x.shape
    depth = w_in.shape[0]
    xf = x.reshape(batch * s, d)
    cosf, sinf = _rope_tables(positions)
    for l in range(depth):
        proj = _in_proj(xf, mix_norm_w[l][None, :], _reorder_w_in(w_in[l]))
        y_ret = _retention(proj, cosf, sinf, ret_norm_w[l][None, :], batch)
        y_ssd = _ssd(proj, ssd_conv_w[l], ssd_conv_b[l], ssd_dt_bias[l], ssd_a_log[l], ssd_d[l],
                     ssd_norm_w[l][None, :], batch)
        y_gdn = _gdn(proj, gdn_conv_w[l], gdn_dt_bias[l], gdn_a_log[l], gdn_norm_w[l], batch)
        xf = _out_proj(xf, (y_ret, y_ssd, y_gdn), w_out[l].astype(BF16))
        xf = _mlp(xf, mlp_norm_w[l][None, :], w_up[l].astype(BF16), w_down[l].astype(BF16),
                  final_norm_w[None, :], final_norm=(l == depth - 1))
    return xf.reshape(batch, s, d)
```

```python
import functools
import math

import numpy as np
import jax
import jax.numpy as jnp
from jax import lax
from jax.experimental import pallas as pl
from jax.experimental.pallas import tpu as pltpu

F32 = jnp.float32
BF16 = jnp.bfloat16
HI = lax.Precision.HIGHEST

NORM_EPS = 1e-6
L2_EPS = 1e-6
ROPE_BASE = 10000.0

RET_HEADS = 4
RET_CHUNK = 128
SSD_HEADS = 8
SSD_GROUPS = 2
SSD_CHUNK = 128
GDN_HEADS = 4
GDN_CHUNK = 64
GDN_BASE = 8
CONV_K = 4

LANE = 128
SEG = 512
CONV_PAD = 8
VMEM_LIMIT = 56 * 1024 * 1024

SEG_RQ, SEG_RK, SEG_RV, SEG_RG, SEG_SZ, SEG_SX, SEG_SBC, SEG_GQ, SEG_GK, SEG_GV, SEG_GZ = range(11)
N_SEG = 11
SMALL_COL = N_SEG * SEG
D_PROJ = SMALL_COL + LANE
GDN_BETA_LANE = SSD_HEADS
GDN_DECAY_LANE = SSD_HEADS + GDN_HEADS


def _dot(a, b, precision=None):
    return jnp.dot(a, b, preferred_element_type=F32, precision=precision)


def _dot_nt(a, b, precision=None):
    return lax.dot_general(a, b, (((1,), (1,)), ((), ())), preferred_element_type=F32, precision=precision)


def _dot_tn(a, b, precision=None):
    return lax.dot_general(a, b, (((0,), (0,)), ((), ())), preferred_element_type=F32, precision=precision)


def _split3(x):
    hi = x.astype(BF16)
    r = x - hi.astype(F32)
    mid = r.astype(BF16)
    lo = (r - mid.astype(F32)).astype(BF16)
    return hi, mid, lo


def _cumsum_pair(tri3, tri3_t, x):
    parts = jnp.concatenate(_split3(x), axis=0)
    return _dot(tri3, parts), _dot_tn(parts, tri3_t)


def _expand_heads(x, e3_ref, nheads):
    hi, mid, lo = _split3(x)
    lane = lax.broadcasted_iota(jnp.int32, x.shape, 1)
    packed = jnp.where(lane < nheads, hi.astype(F32),
                       jnp.where(lane < 2 * nheads, pltpu.roll(mid.astype(F32), nheads, 1),
                                 jnp.where(lane < 3 * nheads, pltpu.roll(lo.astype(F32), 2 * nheads, 1), 0.0)))
    return _dot(packed.astype(BF16), e3_ref[...])


def _silu(t):
    return t * jax.nn.sigmoid(t)


def _softplus(t):
    return jnp.maximum(t, 0.0) + jnp.log1p(jnp.exp(-jnp.abs(t)))


def _rms_rows(t, eps=NORM_EPS):
    return t * lax.rsqrt(jnp.mean(t * t, axis=-1, keepdims=True) + eps)


def _params(*sem):
    return pltpu.CompilerParams(dimension_semantics=sem, vmem_limit_bytes=VMEM_LIMIT)


def _rope_kernel(pos_ref, freq_ref, cos_ref, sin_ref):
    ang = pos_ref[...].astype(F32) * freq_ref[...]
    lane = lax.broadcasted_iota(jnp.int32, ang.shape, 1)
    cos_ref[...] = jnp.cos(ang)
    sin_ref[...] = jnp.where(lane < LANE // 2, -jnp.sin(ang), jnp.sin(ang))


def _rope_tables(positions, tb=1024):
    t = positions.size
    half = LANE // 2
    inv_freq = ROPE_BASE ** (-jnp.arange(half, dtype=F32) / half)
    freq = jnp.concatenate([inv_freq, inv_freq])[None, :]
    tb = min(tb, t)
    return pl.pallas_call(
        _rope_kernel,
        grid=(t // tb,),
        in_specs=[pl.BlockSpec((tb, 1), lambda i: (i, 0)), pl.BlockSpec((1, LANE), lambda i: (0, 0))],
        out_specs=[pl.BlockSpec((tb, LANE), lambda i: (i, 0))] * 2,
        out_shape=[jax.ShapeDtypeStruct((t, LANE), F32)] * 2,
        compiler_params=_params("parallel"),
        name="rope_tables",
    )(positions.reshape(t, 1), freq)


def _inproj_kernel(x_ref, nw_ref, w_ref, o_ref, xn_ref):
    @pl.when(pl.program_id(1) == 0)
    def _():
        xn_ref[...] = (_rms_rows(x_ref[...]) * nw_ref[...]).astype(BF16)

    o_ref[...] = _dot(xn_ref[...], w_ref[...])


def _in_proj(x, norm_w, w, tm=1024, tn=1152):
    t, d = x.shape
    n = w.shape[1]
    tm = min(tm, t)
    return pl.pallas_call(
        _inproj_kernel,
        grid=(t // tm, n // tn),
        in_specs=[pl.BlockSpec((tm, d), lambda i, j: (i, 0)),
                  pl.BlockSpec((1, d), lambda i, j: (0, 0)),
                  pl.BlockSpec((d, tn), lambda i, j: (0, j))],
        out_specs=pl.BlockSpec((tm, tn), lambda i, j: (i, j)),
        out_shape=jax.ShapeDtypeStruct((t, n), F32),
        scratch_shapes=[pltpu.VMEM((tm, d), BF16)],
        compiler_params=_params("parallel", "arbitrary"),
        name="in_proj",
    )(x, norm_w, w)


def _ret_consts():
    c = RET_CHUNK
    h = RET_HEADS
    log_gamma = jnp.log1p(-jnp.exp2(-5.0 - jnp.arange(h, dtype=F32)))
    idx = jnp.arange(c, dtype=F32)
    rel = idx[:, None] - idx[None, :]
    causal = rel >= 0
    d_intra = jnp.where(causal, jnp.exp(log_gamma[:, None, None] * jnp.where(causal, rel, 0.0)), 0.0)
    zeta = jnp.exp(log_gamma[:, None] * (c - 1 - idx))
    xi = jnp.exp(log_gamma[:, None] * (idx + 1))
    cdec = jnp.exp(log_gamma * c)
    ones = jnp.ones((h, c, LANE), F32)
    tab = jnp.stack([zeta[:, :, None] * ones, xi[:, :, None] * ones, cdec[:, None, None] * ones], axis=1)
    return d_intra, tab


def _ret_kernel(q_ref, k_ref, v_ref, g_ref, cos_ref, sin_ref, dm_ref, tab_ref, nw_ref, o_ref, st_ref, *, nchunk):
    c = RET_CHUNK

    @pl.when(pl.program_id(1) == 0)
    def _():
        st_ref[...] = jnp.zeros_like(st_ref)

    scale = LANE ** -0.5
    work = []
    for ci in range(nchunk):
        rows = pl.ds(ci * c, c)
        cos = cos_ref[rows, :]
        sin = sin_ref[rows, :]
        for h in range(RET_HEADS):
            cols = pl.ds(h * LANE, LANE)
            q = q_ref[rows, cols]
            k = k_ref[rows, cols]
            vb = v_ref[rows, cols].astype(BF16)
            q = (q * cos + pltpu.roll(q, LANE // 2, 1) * sin) * scale
            k = k * cos + pltpu.roll(k, LANE // 2, 1) * sin
            qb = q.astype(BF16)
            scores = _dot_nt(qb, k.astype(BF16)) * dm_ref[h]
            work.append(dict(qb=qb, y=_dot(scores.astype(BF16), vb),
                             kv=_dot_tn((k * tab_ref[h, 0]).astype(BF16), vb)))
    states = [st_ref[h] for h in range(RET_HEADS)]
    for ci in range(nchunk):
        rows = pl.ds(ci * c, c)
        for h in range(RET_HEADS):
            cols = pl.ds(h * LANE, LANE)
            wk = work[ci * RET_HEADS + h]
            y = wk["y"] + _dot(wk["qb"], states[h].astype(BF16)) * tab_ref[h, 1]
            states[h] = tab_ref[h, 2] * states[h] + wk["kv"]
            o_ref[rows, cols] = _rms_rows(y) * nw_ref[:, cols] * _silu(g_ref[rows, cols])
    for h in range(RET_HEADS):
        st_ref[h] = states[h]


def _retention(proj, cosf, sinf, norm_w, batch, tb=512):
    t = proj.shape[0]
    s = t // batch
    tb = min(tb, s)
    nb = s // tb
    d_intra, tab = _ret_consts()
    seg = lambda j: pl.BlockSpec((tb, SEG), lambda b, i, j=j: (b * nb + i, j))
    row = pl.BlockSpec((tb, LANE), lambda b, i: (b * nb + i, 0))
    full = lambda a: pl.BlockSpec(a.shape, lambda b, i: (0,) * a.ndim)
    return pl.pallas_call(
        functools.partial(_ret_kernel, nchunk=tb // RET_CHUNK),
        grid=(batch, nb),
        in_specs=[seg(SEG_RQ), seg(SEG_RK), seg(SEG_RV), seg(SEG_RG), row, row,
                  full(d_intra), full(tab), full(norm_w)],
        out_specs=pl.BlockSpec((tb, SEG), lambda b, i: (b * nb + i, 0)),
        out_shape=jax.ShapeDtypeStruct((t, SEG), F32),
        scratch_shapes=[pltpu.VMEM((RET_HEADS, LANE, LANE), F32)],
        compiler_params=_params("parallel", "arbitrary"),
        name="retention",
    )(proj, proj, proj, proj, cosf, sinf, d_intra, tab, norm_w)


def _causal_conv(x_ref, ext_ref, w_ref, first):
    tb = x_ref.shape[0]

    @pl.when(first)
    def _():
        ext_ref[pl.ds(0, CONV_PAD), :] = jnp.zeros((CONV_PAD, ext_ref.shape[1]), F32)

    x = x_ref[...]
    ext_ref[pl.ds(CONV_PAD, tb), :] = x
    acc = x * w_ref[CONV_K - 1:CONV_K, :]
    for j in range(CONV_K - 1):
        shift = CONV_K - 1 - j
        acc = acc + ext_ref[pl.ds(CONV_PAD - shift, tb), :] * w_ref[j:j + 1, :]
    ext_ref[pl.ds(0, CONV_PAD), :] = x[tb - CONV_PAD:, :]
    return acc


def _tri(n, strict=False, upper=False):
    r = lax.broadcasted_iota(jnp.int32, (n, n), 0)
    c = lax.broadcasted_iota(jnp.int32, (n, n), 1)
    if upper:
        r, c = c, r
    return (r > c) if strict else (r >= c)


def _ssd_kernel(z_ref, x_ref, bc_ref, sm_ref, cwx_ref, cwbc_ref, cbx_ref, cbbc_ref, hp_ref, dfull_ref, e_ref,
                nw_ref, o_ref, extx_ref, extbc_ref, st_ref, *, nchunk):
    c = SSD_CHUNK
    nh = SSD_HEADS
    first = pl.program_id(1) == 0

    @pl.when(first)
    def _():
        st_ref[...] = jnp.zeros_like(st_ref)

    xs_all = _silu(_causal_conv(x_ref, extx_ref, cwx_ref, first) + cbx_ref[...])
    bc_all = _silu(_causal_conv(bc_ref, extbc_ref, cwbc_ref, first) + cbbc_ref[...])
    dt_bias = hp_ref[0:1, :]
    a_neg = -jnp.exp(hp_ref[1:2, :])
    d_full = dfull_ref[...]
    causal = _tri(c)
    tri3 = jnp.concatenate([causal.astype(BF16)] * 3, axis=1)
    tri3_t = jnp.concatenate([_tri(c, upper=True).astype(BF16)] * 3, axis=0)
    lane = lax.broadcasted_iota(jnp.int32, (c, LANE), 1)
    gw = SEG // SSD_GROUPS
    ks = LANE
    states = [st_ref[g] for g in range(SSD_GROUPS)]

    for ci in range(nchunk):
        r0 = ci * c
        xs = xs_all[r0:r0 + c, :]
        bc = bc_all[r0:r0 + c, :]
        dt = _softplus(sm_ref[pl.ds(r0, c), :] + dt_bias)
        acs_col, acs_row = _cumsum_pair(tri3, tri3_t, dt * a_neg)
        a_last = acs_col[c - 1:c, :]
        dt_full = _expand_heads(dt, e_ref, nh)
        ea_full = _expand_heads(jnp.exp(acs_col), e_ref, nh)
        te_full = _expand_heads(jnp.exp(a_last - acs_col), e_ref, nh)
        cd_full = ea_full[c - 1:c, :]
        xdt = xs * dt_full
        y_parts = []
        for g in range(SSD_GROUPS):
            bm = bc[:, g * ks:(g + 1) * ks].astype(BF16)
            cm = bc[:, SSD_GROUPS * ks + g * ks:SSD_GROUPS * ks + (g + 1) * ks].astype(BF16)
            cb = _dot_nt(cm, bm)
            gcols = slice(g * gw, (g + 1) * gw)
            prev = states[g]
            y_inter = _dot(cm, prev.astype(BF16)) * ea_full[:, gcols]
            states[g] = cd_full[:, gcols] * prev + _dot_tn(bm, (xdt[:, gcols] * te_full[:, gcols]).astype(BF16))
            for pair in range(gw // LANE):
                ms = []
                for sub in range(2):
                    hd = g * (nh // SSD_GROUPS) + pair * 2 + sub
                    seg = acs_col[:, hd:hd + 1] - acs_row[hd:hd + 1, :]
                    dec = jnp.where(causal, jnp.exp(jnp.where(causal, seg, 0.0)), 0.0)
                    ms.append((cb * dec).astype(BF16))
                xp = xdt[:, g * gw + pair * LANE:g * gw + (pair + 1) * LANE]
                lo = jnp.where(lane < LANE // 2, xp, 0.0).astype(BF16)
                hi = jnp.where(lane >= LANE // 2, xp, 0.0).astype(BF16)
                y_pair = _dot(jnp.concatenate(ms, axis=1), jnp.concatenate([lo, hi], axis=0))
                y_parts.append(y_pair + y_inter[:, pair * LANE:(pair + 1) * LANE])
        y = jnp.concatenate(y_parts, axis=1) + xs * d_full
        y = y * _silu(z_ref[pl.ds(r0, c), :])
        outs = [_rms_rows(y[:, g * gw:(g + 1) * gw]) for g in range(SSD_GROUPS)]
        o_ref[pl.ds(r0, c), :] = jnp.concatenate(outs, axis=1) * nw_ref[...]
    for g in range(SSD_GROUPS):
        st_ref[g] = states[g]


def _pad_rows(a, rows=8, cols=LANE):
    out = jnp.zeros((rows, cols), F32)
    return out.at[:a.shape[0], :a.shape[1]].set(a.astype(F32))


def _ssd(proj, conv_w, conv_b, dt_bias, a_log, d_skip, norm_w, batch, tb=512):
    t = proj.shape[0]
    s = t // batch
    tb = min(tb, s)
    nb = s // tb
    hp = _pad_rows(jnp.stack([dt_bias, a_log]))
    hw = SEG // SSD_HEADS
    d_full = jnp.repeat(d_skip.astype(F32), hw)[None, :]
    expand = jnp.tile(jnp.repeat(jnp.eye(SSD_HEADS, dtype=BF16), hw, axis=1), (3, 1))
    expand = jnp.pad(expand, ((0, LANE - 3 * SSD_HEADS), (0, 0)))
    cwx, cwbc = conv_w[:, :SEG], conv_w[:, SEG:]
    cbx, cbbc = conv_b[None, :SEG], conv_b[None, SEG:]
    seg = lambda j: pl.BlockSpec((tb, SEG), lambda b, i, j=j: (b * nb + i, j))
    small = pl.BlockSpec((tb, LANE), lambda b, i: (b * nb + i, SMALL_COL // LANE))
    full = lambda a: pl.BlockSpec(a.shape, lambda b, i: (0,) * a.ndim)
    consts = [cwx, cwbc, cbx, cbbc, hp, d_full, expand, norm_w]
    return pl.pallas_call(
        functools.partial(_ssd_kernel, nchunk=tb // SSD_CHUNK),
        grid=(batch, nb),
        in_specs=[seg(SEG_SZ), seg(SEG_SX), seg(SEG_SBC), small] + [full(a) for a in consts],
        out_specs=pl.BlockSpec((tb, SEG), lambda b, i: (b * nb + i, 0)),
        out_shape=jax.ShapeDtypeStruct((t, SEG), F32),
        scratch_shapes=[pltpu.VMEM((tb + CONV_PAD, SEG), F32), pltpu.VMEM((tb + CONV_PAD, SEG), F32),
                        pltpu.VMEM((SSD_GROUPS, LANE, SEG // SSD_GROUPS), F32)],
        compiler_params=_params("parallel", "arbitrary"),
        name="ssd",
    )(proj, proj, proj, proj, *consts)


def _gdn_kernel(q_ref, k_ref, v_ref, z_ref, sm_ref, cwq_ref, cwk_ref, cwv_ref, hp_ref, nw_ref,
                o_ref, extq_ref, extk_ref, extv_ref, st_ref, *, nchunk):
    c = GDN_CHUNK
    nh = GDN_HEADS
    first = pl.program_id(1) == 0

    @pl.when(first)
    def _():
        st_ref[...] = jnp.zeros_like(st_ref)

    q_all = _silu(_causal_conv(q_ref, extq_ref, cwq_ref, first))
    k_all = _silu(_causal_conv(k_ref, extk_ref, cwk_ref, first))
    v_all = _silu(_causal_conv(v_ref, extv_ref, cwv_ref, first))
    dt_bias = hp_ref[0:1, :]
    a_neg = -jnp.exp(hp_ref[1:2, :])
    incl = _tri(c)
    strict = _tri(c, strict=True)
    scale = LANE ** -0.5
    tb = nchunk * c

    sm = sm_ref[...]
    beta = jax.nn.sigmoid(sm)
    g = a_neg * _softplus(sm + dt_bias)
    tri3 = jnp.concatenate([incl.astype(BF16)] * 3, axis=1)
    tri3_t = jnp.concatenate([_tri(c, upper=True).astype(BF16)] * 3, axis=0)

    work = []
    for ci in range(nchunk):
        r0 = ci * c
        gcs_col, gcs_row = _cumsum_pair(tri3, tri3_t, g[r0:r0 + c, :])
        g_last = gcs_col[c - 1:c, :]
        eg_all = jnp.exp(gcs_col)
        ekd_all = jnp.exp(g_last - gcs_col)
        cdec_all = jnp.exp(g_last)
        for h in range(nh):
            cols = slice(h * LANE, (h + 1) * LANE)
            dl = GDN_DECAY_LANE + h
            eg = eg_all[:, dl:dl + 1]
            q = q_all[r0:r0 + c, cols]
            k = k_all[r0:r0 + c, cols]
            v = v_all[r0:r0 + c, cols]
            q = q * lax.rsqrt(jnp.sum(q * q, axis=-1, keepdims=True) + L2_EPS) * scale
            k = k * lax.rsqrt(jnp.sum(k * k, axis=-1, keepdims=True) + L2_EPS)
            bh = beta[r0:r0 + c, GDN_BETA_LANE + h:GDN_BETA_LANE + h + 1]
            kb = k * bh
            diff = gcs_col[:, dl:dl + 1] - gcs_row[dl:dl + 1, :]
            decay = jnp.where(incl, jnp.exp(jnp.where(incl, diff, 0.0)), 0.0)
            kq = _dot_nt(jnp.concatenate([kb, q], axis=0).astype(BF16), k.astype(BF16))
            work.append(dict(
                low=jnp.where(strict, kq[:c] * decay, 0.0),
                attn=(kq[c:] * decay).astype(BF16),
                sol=jnp.concatenate([v * bh, kb * eg], axis=1),
                qd=(q * eg).astype(BF16),
                kd=(k * ekd_all[:, dl:dl + 1]).astype(BF16),
                cdec=cdec_all[:, dl:dl + 1]))

    rb = lax.broadcasted_iota(jnp.int32, (c, c), 0)
    cb = lax.broadcasted_iota(jnp.int32, (c, c), 1)
    same = lambda n: (rb // n) == (cb // n)
    eye = (rb == cb).astype(F32)
    for wk in work:
        wk["n"] = jnp.where(same(GDN_BASE), -wk["low"], 0.0).astype(BF16)
    for wk in work:
        wk["n2"] = _dot(wk["n"], wk["n"]).astype(BF16)
        wk["t"] = eye + wk["n"].astype(F32)
    for wk in work:
        wk["t"] = wk["t"] + _dot(wk["t"].astype(BF16), wk["n2"])
        wk["n4"] = _dot(wk["n2"], wk["n2"]).astype(BF16)
    for wk in work:
        wk["t"] = wk["t"] + _dot(wk["t"].astype(BF16), wk["n4"])
    n = GDN_BASE
    while n < c:
        off = same(2 * n) & jnp.logical_not(same(n))
        for wk in work:
            tb16 = wk["t"].astype(BF16)
            wk["tc"] = (_dot(tb16, jnp.where(off, wk["low"], 0.0).astype(BF16)).astype(BF16), tb16)
        for wk in work:
            tc, tb16 = wk["tc"]
            wk["t"] = wk["t"] - _dot(tc, tb16)
        n *= 2
    for wk in work:
        wk["sol"] = _dot(wk["t"].astype(BF16), wk["sol"].astype(BF16))

    states = [st_ref[h] for h in range(nh)]
    for ci in range(nchunk):
        rows = pl.ds(ci * c, c)
        for h in range(nh):
            cols = slice(h * LANE, (h + 1) * LANE)
            wk = work[ci * nh + h]
            st = states[h]
            ws = _dot(jnp.concatenate([wk["sol"][:, LANE:].astype(BF16), wk["qd"]], axis=0), st.astype(BF16))
            v_new = (wk["sol"][:, :LANE] - ws[:c]).astype(BF16)
            o = ws[c:] + _dot(wk["attn"], v_new)
            states[h] = st * wk["cdec"] + _dot_tn(wk["kd"], v_new)
            o_ref[rows, cols] = _rms_rows(o) * nw_ref[...] * _silu(z_ref[rows, cols])
    for h in range(nh):
        st_ref[h] = states[h]


def _gdn(proj, conv_w, dt_bias, a_log, norm_w, batch, tb=512):
    t = proj.shape[0]
    s = t // batch
    tb = min(tb, s)
    nb = s // tb
    hp = jnp.zeros((8, LANE), F32).at[0:2, GDN_DECAY_LANE:GDN_DECAY_LANE + GDN_HEADS].set(
        jnp.stack([dt_bias, a_log]).astype(F32))
    cws =[conv_w[:, i * SEG:(i + 1) * SEG] for i in range(3)]
    seg = lambda j: pl.BlockSpec((tb, SEG), lambda b, i, j=j: (b * nb + i, j))
    small = pl.BlockSpec((tb, LANE), lambda b, i: (b * nb + i, SMALL_COL // LANE))
    full = lambda a: pl.BlockSpec(a.shape, lambda b, i: (0,) * a.ndim)
    consts = cws + [hp, norm_w[None, :]]
    return pl.pallas_call(
        functools.partial(_gdn_kernel, nchunk=tb // GDN_CHUNK),
        grid=(batch, nb),
        in_specs=[seg(SEG_GQ), seg(SEG_GK), seg(SEG_GV), seg(SEG_GZ), small] + [full(a) for a in consts],
        out_specs=pl.BlockSpec((tb, SEG), lambda b, i: (b * nb + i, 0)),
        out_shape=jax.ShapeDtypeStruct((t, SEG), F32),
        scratch_shapes=[pltpu.VMEM((tb + CONV_PAD, SEG), F32)] * 3 + [pltpu.VMEM((GDN_HEADS, LANE, LANE), F32)],
        compiler_params=_params("parallel", "arbitrary"),
        name="gdn",
    )(proj, proj, proj, proj, proj, *consts)


def _outproj_kernel(x_ref, y0_ref, y1_ref, y2_ref, w_ref, o_ref):
    acc = x_ref[...]
    for i, y_ref in enumerate((y0_ref, y1_ref, y2_ref)):
        acc = acc + _dot(y_ref[...].astype(BF16), w_ref[pl.ds(i * SEG, SEG), :])
    o_ref[...] = acc


def _out_proj(x, ys, w, tm=512):
    t, d = x.shape
    tm = min(tm, t)
    return pl.pallas_call(
        _outproj_kernel,
        grid=(t // tm,),
        in_specs=[pl.BlockSpec((tm, d), lambda i: (i, 0))] + [pl.BlockSpec((tm, SEG), lambda i: (i, 0))] * 3
                 + [pl.BlockSpec(w.shape, lambda i: (0, 0))],
        out_specs=pl.BlockSpec((tm, d), lambda i: (i, 0)),
        out_shape=jax.ShapeDtypeStruct((t, d), F32),
        compiler_params=_params("parallel"),
        name="out_proj",
    )(x, *ys, w)


def _mlp_kernel(x_ref, nw_ref, wu_ref, wd_ref, fw_ref, o_ref, xn_ref, acc_ref, *, final_norm):
    j = pl.program_id(1)

    @pl.when(j == 0)
    def _():
        xn_ref[...] = (_rms_rows(x_ref[...]) * nw_ref[...]).astype(BF16)
        acc_ref[...] = jnp.zeros_like(acc_ref)

    h = jnp.maximum(_dot(xn_ref[...], wu_ref[...]), 0.0)
    acc_ref[...] += _dot((h * h).astype(BF16), wd_ref[...])

    @pl.when(j == pl.num_programs(1) - 1)
    def _():
        out = x_ref[...] + acc_ref[...]
        if final_norm:
            out = _rms_rows(out) * fw_ref[...]
        o_ref[...] = out


def _mlp(x, norm_w, w_up, w_down, final_w, final_norm, tm=1024, tf=1024):
    t, d = x.shape
    ff = w_up.shape[1]
    tm = min(tm, t)
    return pl.pallas_call(
        functools.partial(_mlp_kernel, final_norm=final_norm),
        grid=(t // tm, ff // tf),
        in_specs=[pl.BlockSpec((tm, d), lambda i, j: (i, 0)),
                  pl.BlockSpec((1, d), lambda i, j: (0, 0)),
                  pl.BlockSpec((d, tf), lambda i, j: (0, j)),
                  pl.BlockSpec((tf, d), lambda i, j: (j, 0)),
                  pl.BlockSpec((1, d), lambda i, j: (0, 0))],
        out_specs=pl.BlockSpec((tm, d), lambda i, j: (i, 0)),
        out_shape=jax.ShapeDtypeStruct((t, d), F32),
        scratch_shapes=[pltpu.VMEM((tm, d), BF16), pltpu.VMEM((tm, d), F32)],
        compiler_params=_params("parallel", "arbitrary"),
        name="mlp",
    )(x, norm_w, w_up, w_down, final_w)


def _reorder_w_in(w):
    sizes = (SEG, SEG, SEG, SEG, SEG, 2 * SEG, SSD_HEADS, 3 * SEG, SEG, GDN_HEADS, GDN_HEADS)
    offs = np.concatenate([[0], np.cumsum(sizes)])
    piece = lambda i: w[:, offs[i]:offs[i + 1]]
    small = jnp.concatenate([piece(6), piece(9), piece(10)], axis=1)
    small = jnp.pad(small, ((0, 0), (0, LANE - small.shape[1])))
    big = [piece(i) for i in (0, 1, 2, 3, 4, 5, 7, 8)]
    return jnp.concatenate(big + [small], axis=1).astype(BF16)


def kernel(x, positions, mix_norm_w, w_in, ret_norm_w, ssd_conv_w, ssd_conv_b, ssd_dt_bias, ssd_a_log, ssd_d,
           ssd_norm_w, gdn_conv_w, gdn_dt_bias, gdn_a_log, gdn_norm_w, w_out, mlp_norm_w, w_up, w_down,
           final_norm_w):
    batch, s, d = x.shape
    depth = w_in.shape[0]
    xf = x.reshape(batch * s, d)
    cosf, sinf = _rope_tables(positions)
    for l in range(depth):
        proj = _in_proj(xf, mix_norm_w[l][None, :], _reorder_w_in(w_in[l]))
        y_ret = _retention(proj, cosf, sinf, ret_norm_w[l][None, :], batch)
        y_ssd = _ssd(proj, ssd_conv_w[l], ssd_conv_b[l], ssd_dt_bias[l], ssd_a_log[l], ssd_d[l],
                     ssd_norm_w[l][None, :], batch)
        y_gdn = _gdn(proj, gdn_conv_w[l], gdn_dt_bias[l], gdn_a_log[l], gdn_norm_w[l], batch)
        xf = _out_proj(xf, (y_ret, y_ssd, y_gdn), w_out[l].astype(BF16))
        xf = _mlp(xf, mlp_norm_w[l][None, :], w_up[l].astype(BF16), w_down[l].astype(BF16),
                  final_norm_w[None, :], final_norm=(l == depth - 1))
    return xf.reshape(batch, s, d)
```

```python
import functools
import math

import numpy as np
import jax
import jax.numpy as jnp
from jax import lax
from jax.experimental import pallas as pl
from jax.experimental.pallas import tpu as pltpu

F32 = jnp.float32
BF16 = jnp.bfloat16
HI = lax.Precision.HIGHEST

NORM_EPS = 1e-6
L2_EPS = 1e-6
ROPE_BASE = 10000.0

RET_HEADS = 4
RET_CHUNK = 128
SSD_HEADS = 8
SSD_GROUPS = 2
SSD_CHUNK = 128
GDN_HEADS = 4
GDN_CHUNK = 64
GDN_BASE = 8
CONV_K = 4

LANE = 128
SEG = 512
CONV_PAD = 8
VMEM_LIMIT = 56 * 1024 * 1024

SEG_RQ, SEG_RK, SEG_RV, SEG_RG, SEG_SZ, SEG_SX, SEG_SBC, SEG_GQ, SEG_GK, SEG_GV, SEG_GZ = range(11)
N_SEG = 11
SMALL_COL = N_SEG * SEG
D_PROJ = SMALL_COL + LANE
GDN_BETA_LANE = SSD_HEADS
GDN_DECAY_LANE = SSD_HEADS + GDN_HEADS


def _dot(a, b, precision=None):
    return jnp.dot(a, b, preferred_element_type=F32, precision=precision)


def _dot_nt(a, b, precision=None):
    return lax.dot_general(a, b, (((1,), (1,)), ((), ())), preferred_element_type=F32, precision=precision)


def _dot_tn(a, b, precision=None):
    return lax.dot_general(a, b, (((0,), (0,)), ((), ())), preferred_element_type=F32, precision=precision)


def _split3(x):
    hi = x.astype(BF16)
    r = x - hi.astype(F32)
    mid = r.astype(BF16)
    lo = (r - mid.astype(F32)).astype(BF16)
    return hi, mid, lo


def _cumsum_pair(tri3, tri3_t, x):
    parts = jnp.concatenate(_split3(x), axis=0)
    return _dot(tri3, parts), _dot_tn(parts, tri3_t)


def _expand_heads(x, e3_ref, nheads):
    hi, mid, lo = _split3(x)
    lane = lax.broadcasted_iota(jnp.int32, x.shape, 1)
    packed = jnp.where(lane < nheads, hi.astype(F32),
                       jnp.where(lane < 2 * nheads, pltpu.roll(mid.astype(F32), nheads, 1),
                                 jnp.where(lane < 3 * nheads, pltpu.roll(lo.astype(F32), 2 * nheads, 1), 0.0)))
    return _dot(packed.astype(BF16), e3_ref[...])


def _silu(t):
    return t * jax.nn.sigmoid(t)


def _softplus(t):
    return jnp.maximum(t, 0.0) + jnp.log1p(jnp.exp(-jnp.abs(t)))


def _rms_rows(t, eps=NORM_EPS):
    return t * lax.rsqrt(jnp.mean(t * t, axis=-1, keepdims=True) + eps)


def _params(*sem):
    return pltpu.CompilerParams(dimension_semantics=sem, vmem_limit_bytes=VMEM_LIMIT)


def _rope_kernel(pos_ref, freq_ref, cos_ref, sin_ref):
    ang = pos_ref[...].astype(F32) * freq_ref[...]
    lane = lax.broadcasted_iota(jnp.int32, ang.shape, 1)
    cos_ref[...] = jnp.cos(ang)
    sin_ref[...] = jnp.where(lane < LANE // 2, -jnp.sin(ang), jnp.sin(ang))


def _rope_tables(positions, tb=1024):
    t = positions.size
    half = LANE // 2
    inv_freq = ROPE_BASE ** (-jnp.arange(half, dtype=F32) / half)
    freq = jnp.concatenate([inv_freq, inv_freq])[None, :]
    tb = min(tb, t)
    return pl.pallas_call(
        _rope_kernel,
        grid=(t // tb,),
        in_specs=[pl.BlockSpec((tb, 1), lambda i: (i, 0)), pl.BlockSpec((1, LANE), lambda i: (0, 0))],
        out_specs=[pl.BlockSpec((tb, LANE), lambda i: (i, 0))] * 2,
        out_shape=[jax.ShapeDtypeStruct((t, LANE), F32)] * 2,
        compiler_params=_params("parallel"),
        name="rope_tables",
    )(positions.reshape(t, 1), freq)


def _project(x_ref, nw_ref, w_ref, proj_ref):
    xn = (_rms_rows(x_ref[...]) * nw_ref[...]).astype(BF16)
    n = w_ref.shape[1]
    for c0 in range(0, n, SEG):
        width = min(SEG, n - c0)
        proj_ref[:, pl.ds(c0, width)] = _dot(xn, w_ref[:, pl.ds(c0, width)])


def _seg_view(proj_ref, j, width=SEG):
    return proj_ref.at[:, pl.ds(j * SEG, width)]


def _ret_consts():
    c = RET_CHUNK
    h = RET_HEADS
    log_gamma = jnp.log1p(-jnp.exp2(-5.0 - jnp.arange(h, dtype=F32)))
    idx = jnp.arange(c, dtype=F32)
    rel = idx[:, None] - idx[None, :]
    causal = rel >= 0
    d_intra = jnp.where(causal, jnp.exp(log_gamma[:, None, None] * jnp.where(causal, rel, 0.0)), 0.0)
    zeta = jnp.exp(log_gamma[:, None] * (c - 1 - idx))
    xi = jnp.exp(log_gamma[:, None] * (idx + 1))
    cdec = jnp.exp(log_gamma * c)
    ones = jnp.ones((h, c, LANE), F32)
    tab = jnp.stack([zeta[:, :, None] * ones, xi[:, :, None] * ones, cdec[:, None, None] * ones], axis=1)
    return d_intra, tab


def _ret_kernel(x_ref, mnw_ref, w_ref, cos_ref, sin_ref, dm_ref, tab_ref, nw_ref, o_ref, proj_ref, st_ref, *,
                nchunk):
    c = RET_CHUNK

    @pl.when(pl.program_id(1) == 0)
    def _():
        st_ref[...] = jnp.zeros_like(st_ref)

    _project(x_ref, mnw_ref, w_ref, proj_ref)
    q_ref, k_ref, v_ref, g_ref = (_seg_view(proj_ref, j) for j in range(4))

    scale = LANE ** -0.5
    work = []
    for ci in range(nchunk):
        rows = pl.ds(ci * c, c)
        cos = cos_ref[rows, :]
        sin = sin_ref[rows, :]
        for h in range(RET_HEADS):
            cols = pl.ds(h * LANE, LANE)
            q = q_ref[rows, cols]
            k = k_ref[rows, cols]
            vb = v_ref[rows, cols].astype(BF16)
            q = (q * cos + pltpu.roll(q, LANE // 2, 1) * sin) * scale
            k = k * cos + pltpu.roll(k, LANE // 2, 1) * sin
            qb = q.astype(BF16)
            scores = _dot_nt(qb, k.astype(BF16)) * dm_ref[h]
            work.append(dict(qb=qb, y=_dot(scores.astype(BF16), vb),
                             kv=_dot_tn((k * tab_ref[h, 0]).astype(BF16), vb)))
    states = [st_ref[h] for h in range(RET_HEADS)]
    for ci in range(nchunk):
        rows = pl.ds(ci * c, c)
        for h in range(RET_HEADS):
            cols = pl.ds(h * LANE, LANE)
            wk = work[ci * RET_HEADS + h]
            y = wk["y"] + _dot(wk["qb"], states[h].astype(BF16)) * tab_ref[h, 1]
            states[h] = tab_ref[h, 2] * states[h] + wk["kv"]
            o_ref[rows, cols] = (_rms_rows(y) * nw_ref[:, cols] * _silu(g_ref[rows, cols])).astype(BF16)
    for h in range(RET_HEADS):
        st_ref[h] = states[h]


def _mixer_call(body, name, x, mix_norm_w, w, row_inputs, consts, scratch, batch, tb):
    t, d = x.shape
    nb = t // batch // tb
    rows = lambda width: pl.BlockSpec((tb, width), lambda b, i: (b * nb + i, 0))
    full = lambda a: pl.BlockSpec(a.shape, lambda b, i: (0,) * a.ndim)
    in_specs = [rows(d), full(mix_norm_w), full(w)]
    in_specs += [rows(a.shape[1]) for a in row_inputs] + [full(a) for a in consts]
    consts = list(row_inputs) + list(consts)
    return pl.pallas_call(
        body,
        grid=(batch, nb),
        in_specs=in_specs,
        out_specs=rows(SEG),
        out_shape=jax.ShapeDtypeStruct((t, SEG), BF16),
        scratch_shapes=[pltpu.VMEM((tb, w.shape[1]), F32)] + scratch,
        compiler_params=_params("parallel", "arbitrary"),
        name=name,
    )(x, mix_norm_w, w, *consts)


def _retention(x, mix_norm_w, w, cosf, sinf, norm_w, batch, tb=512):
    tb = min(tb, x.shape[0] // batch)
    d_intra, tab = _ret_consts()
    return _mixer_call(functools.partial(_ret_kernel, nchunk=tb // RET_CHUNK), "retention", x, mix_norm_w, w,
                       [cosf, sinf], [d_intra, tab, norm_w], [pltpu.VMEM((RET_HEADS, LANE, LANE), F32)], batch, tb)


def _causal_conv(x_ref, ext_ref, w_ref, first):
    tb = x_ref.shape[0]

    @pl.when(first)
    def _():
        ext_ref[pl.ds(0, CONV_PAD), :] = jnp.zeros((CONV_PAD, ext_ref.shape[1]), F32)

    x = x_ref[...]
    ext_ref[pl.ds(CONV_PAD, tb), :] = x
    acc = x * w_ref[CONV_K - 1:CONV_K, :]
    for j in range(CONV_K - 1):
        shift = CONV_K - 1 - j
        acc = acc + ext_ref[pl.ds(CONV_PAD - shift, tb), :] * w_ref[j:j + 1, :]
    ext_ref[pl.ds(0, CONV_PAD), :] = x[tb - CONV_PAD:, :]
    return acc


def _tri(n, strict=False, upper=False):
    r = lax.broadcasted_iota(jnp.int32, (n, n), 0)
    c = lax.broadcasted_iota(jnp.int32, (n, n), 1)
    if upper:
        r, c = c, r
    return (r > c) if strict else (r >= c)


def _ssd_kernel(xin_ref, mnw_ref, w_ref, cwx_ref, cwbc_ref, cbx_ref, cbbc_ref, hp_ref, dfull_ref, e_ref,
                nw_ref, o_ref, proj_ref, extx_ref, extbc_ref, st_ref, *, nchunk):
    c = SSD_CHUNK
    nh = SSD_HEADS
    first = pl.program_id(1) == 0

    @pl.when(first)
    def _():
        st_ref[...] = jnp.zeros_like(st_ref)

    _project(xin_ref, mnw_ref, w_ref, proj_ref)
    z_ref, x_ref, bc_ref = (_seg_view(proj_ref, j) for j in range(3))
    sm_ref = _seg_view(proj_ref, 3, LANE)

    xs_all = _silu(_causal_conv(x_ref, extx_ref, cwx_ref, first) + cbx_ref[...])
    bc_all = _silu(_causal_conv(bc_ref, extbc_ref, cwbc_ref, first) + cbbc_ref[...])
    dt_bias = hp_ref[0:1, :]
    a_neg = -jnp.exp(hp_ref[1:2, :])
    d_full = dfull_ref[...]
    causal = _tri(c)
    tri3 = jnp.concatenate([causal.astype(BF16)] * 3, axis=1)
    tri3_t = jnp.concatenate([_tri(c, upper=True).astype(BF16)] * 3, axis=0)
    lane = lax.broadcasted_iota(jnp.int32, (c, LANE), 1)
    gw = SEG // SSD_GROUPS
    ks = LANE
    states = [st_ref[g] for g in range(SSD_GROUPS)]

    for ci in range(nchunk):
        r0 = ci * c
        xs = xs_all[r0:r0 + c, :]
        bc = bc_all[r0:r0 + c, :]
        dt = _softplus(sm_ref[pl.ds(r0, c), :] + dt_bias)
        acs_col, acs_row = _cumsum_pair(tri3, tri3_t, dt * a_neg)
        a_last = acs_col[c - 1:c, :]
        dt_full = _expand_heads(dt, e_ref, nh)
        ea_full = _expand_heads(jnp.exp(acs_col), e_ref, nh)
        te_full = _expand_heads(jnp.exp(a_last - acs_col), e_ref, nh)
        cd_full = ea_full[c - 1:c, :]
        xdt = xs * dt_full
        y_parts = []
        for g in range(SSD_GROUPS):
            bm = bc[:, g * ks:(g + 1) * ks].astype(BF16)
            cm = bc[:, SSD_GROUPS * ks + g * ks:SSD_GROUPS * ks + (g + 1) * ks].astype(BF16)
            cb = _dot_nt(cm, bm)
            gcols = slice(g * gw, (g + 1) * gw)
            prev = states[g]
            y_inter = _dot(cm, prev.astype(BF16)) * ea_full[:, gcols]
            states[g] = cd_full[:, gcols] * prev + _dot_tn(bm, (xdt[:, gcols] * te_full[:, gcols]).astype(BF16))
            for pair in range(gw // LANE):
                ms = []
                for sub in range(2):
                    hd = g * (nh // SSD_GROUPS) + pair * 2 + sub
                    seg = acs_col[:, hd:hd + 1] - acs_row[hd:hd + 1, :]
                    dec = jnp.where(causal, jnp.exp(jnp.where(causal, seg, 0.0)), 0.0)
                    ms.append((cb * dec).astype(BF16))
                xp = xdt[:, g * gw + pair * LANE:g * gw + (pair + 1) * LANE]
                lo = jnp.where(lane < LANE // 2, xp, 0.0).astype(BF16)
                hi = jnp.where(lane >= LANE // 2, xp, 0.0).astype(BF16)
                y_pair = _dot(jnp.concatenate(ms, axis=1), jnp.concatenate([lo, hi], axis=0))
                y_parts.append(y_pair + y_inter[:, pair * LANE:(pair + 1) * LANE])
        y = jnp.concatenate(y_parts, axis=1) + xs * d_full
        y = y * _silu(z_ref[pl.ds(r0, c), :])
        outs = [_rms_rows(y[:, g * gw:(g + 1) * gw]) for g in range(SSD_GROUPS)]
        o_ref[pl.ds(r0, c), :] = (jnp.concatenate(outs, axis=1) * nw_ref[...]).astype(BF16)
    for g in range(SSD_GROUPS):
        st_ref[g] = states[g]


def _pad_rows(a, rows=8, cols=LANE):
    out = jnp.zeros((rows, cols), F32)
    return out.at[:a.shape[0], :a.shape[1]].set(a.astype(F32))


def _ssd(x, mix_norm_w, w, conv_w, conv_b, dt_bias, a_log, d_skip, norm_w, batch, tb=512):
    tb = min(tb, x.shape[0] // batch)
    hp = _pad_rows(jnp.stack([dt_bias, a_log]))
    hw = SEG // SSD_HEADS
    d_full = jnp.repeat(d_skip.astype(F32), hw)[None, :]
    expand = jnp.tile(jnp.repeat(jnp.eye(SSD_HEADS, dtype=BF16), hw, axis=1), (3, 1))
    expand = jnp.pad(expand, ((0, LANE - 3 * SSD_HEADS), (0, 0)))
    cwx, cwbc = conv_w[:, :SEG], conv_w[:, SEG:]
    cbx, cbbc = conv_b[None, :SEG], conv_b[None, SEG:]
    consts = [cwx, cwbc, cbx, cbbc, hp, d_full, expand, norm_w]
    scratch = [pltpu.VMEM((tb + CONV_PAD, SEG), F32), pltpu.VMEM((tb + CONV_PAD, SEG), F32),
               pltpu.VMEM((SSD_GROUPS, LANE, SEG // SSD_GROUPS), F32)]
    return _mixer_call(functools.partial(_ssd_kernel, nchunk=tb // SSD_CHUNK), "ssd", x, mix_norm_w, w,
                       [], consts, scratch, batch, tb)


def _gdn_kernel(xin_ref, mnw_ref, w_ref, cwq_ref, cwk_ref, cwv_ref, hp_ref, nw_ref,
                o_ref, proj_ref, extq_ref, extk_ref, extv_ref, st_ref, *, nchunk):
    c = GDN_CHUNK
    nh = GDN_HEADS
    first = pl.program_id(1) == 0

    @pl.when(first)
    def _():
        st_ref[...] = jnp.zeros_like(st_ref)

    _project(xin_ref, mnw_ref, w_ref, proj_ref)
    q_ref, k_ref, v_ref, z_ref = (_seg_view(proj_ref, j) for j in range(4))
    sm_ref = _seg_view(proj_ref, 4, LANE)

    q_all = _silu(_causal_conv(q_ref, extq_ref, cwq_ref, first))
    k_all = _silu(_causal_conv(k_ref, extk_ref, cwk_ref, first))
    v_all = _silu(_causal_conv(v_ref, extv_ref, cwv_ref, first))
    dt_bias = hp_ref[0:1, :]
    a_neg = -jnp.exp(hp_ref[1:2, :])
    incl = _tri(c)
    strict = _tri(c, strict=True)
    scale = LANE ** -0.5
    tb = nchunk * c

    sm = sm_ref[...]
    beta = jax.nn.sigmoid(sm)
    g = a_neg * _softplus(sm + dt_bias)
    tri3 = jnp.concatenate([incl.astype(BF16)] * 3, axis=1)
    tri3_t = jnp.concatenate([_tri(c, upper=True).astype(BF16)] * 3, axis=0)

    work = []
    for ci in range(nchunk):
        r0 = ci * c
        gcs_col, gcs_row = _cumsum_pair(tri3, tri3_t, g[r0:r0 + c, :])
        g_last = gcs_col[c - 1:c, :]
        eg_all = jnp.exp(gcs_col)
        ekd_all = jnp.exp(g_last - gcs_col)
        cdec_all = jnp.exp(g_last)
        for h in range(nh):
            cols = slice(h * LANE, (h + 1) * LANE)
            dl = GDN_DECAY_LANE + h
            eg = eg_all[:, dl:dl + 1]
            q = q_all[r0:r0 + c, cols]
            k = k_all[r0:r0 + c, cols]
            v = v_all[r0:r0 + c, cols]
            q = q * lax.rsqrt(jnp.sum(q * q, axis=-1, keepdims=True) + L2_EPS) * scale
            k = k * lax.rsqrt(jnp.sum(k * k, axis=-1, keepdims=True) + L2_EPS)
            bh = beta[r0:r0 + c, GDN_BETA_LANE + h:GDN_BETA_LANE + h + 1]
            kb = k * bh
            diff = gcs_col[:, dl:dl + 1] - gcs_row[dl:dl + 1, :]
            decay = jnp.where(incl, jnp.exp(jnp.where(incl, diff, 0.0)), 0.0)
            kq = _dot_nt(jnp.concatenate([kb, q], axis=0).astype(BF16), k.astype(BF16))
            work.append(dict(
                low=jnp.where(strict, kq[:c] * decay, 0.0),
                attn=(kq[c:] * decay).astype(BF16),
                sol=jnp.concatenate([v * bh, kb * eg], axis=1),
                qd=(q * eg).astype(BF16),
                kd=(k * ekd_all[:, dl:dl + 1]).astype(BF16),
                cdec=cdec_all[:, dl:dl + 1]))

    rb = lax.broadcasted_iota(jnp.int32, (c, c), 0)
    cb = lax.broadcasted_iota(jnp.int32, (c, c), 1)
    same = lambda n: (rb // n) == (cb // n)
    eye = (rb == cb).astype(F32)
    for wk in work:
        wk["n"] = jnp.where(same(GDN_BASE), -wk["low"], 0.0).astype(BF16)
    for wk in work:
        wk["n2"] = _dot(wk["n"], wk["n"]).astype(BF16)
        wk["t"] = eye + wk["n"].astype(F32)
    for wk in work:
        wk["t"] = wk["t"] + _dot(wk["t"].astype(BF16), wk["n2"])
        wk["n4"] = _dot(wk["n2"], wk["n2"]).astype(BF16)
    for wk in work:
        wk["t"] = wk["t"] + _dot(wk["t"].astype(BF16), wk["n4"])
    n = GDN_BASE
    while n < c:
        off = same(2 * n) & jnp.logical_not(same(n))
        for wk in work:
            tb16 = wk["t"].astype(BF16)
            wk["tc"] = (_dot(tb16, jnp.where(off, wk["low"], 0.0).astype(BF16)).astype(BF16), tb16)
        for wk in work:
            tc, tb16 = wk["tc"]
            wk["t"] = wk["t"] - _dot(tc, tb16)
        n *= 2
    for wk in work:
        wk["sol"] = _dot(wk["t"].astype(BF16), wk["sol"].astype(BF16))

    states = [st_ref[h] for h in range(nh)]
    for ci in range(nchunk):
        rows = pl.ds(ci * c, c)
        for h in range(nh):
            cols = slice(h * LANE, (h + 1) * LANE)
            wk = work[ci * nh + h]
            st = states[h]
            ws = _dot(jnp.concatenate([wk["sol"][:, LANE:].astype(BF16), wk["qd"]], axis=0), st.astype(BF16))
            v_new = (wk["sol"][:, :LANE] - ws[:c]).astype(BF16)
            o = ws[c:] + _dot(wk["attn"], v_new)
            states[h] = st * wk["cdec"] + _dot_tn(wk["kd"], v_new)
            o_ref[rows, cols] = (_rms_rows(o) * nw_ref[...] * _silu(z_ref[rows, cols])).astype(BF16)
    for h in range(nh):
        st_ref[h] = states[h]


def _gdn(x, mix_norm_w, w, conv_w, dt_bias, a_log, norm_w, batch, tb=512):
    tb = min(tb, x.shape[0] // batch)
    hp = jnp.zeros((8, LANE), F32).at[0:2, GDN_DECAY_LANE:GDN_DECAY_LANE + GDN_HEADS].set(
        jnp.stack([dt_bias, a_log]).astype(F32))
    cws = [conv_w[:, i * SEG:(i + 1) * SEG] for i in range(3)]
    scratch = [pltpu.VMEM((tb + CONV_PAD, SEG), F32)] * 3 + [pltpu.VMEM((GDN_HEADS, LANE, LANE), F32)]
    return _mixer_call(functools.partial(_gdn_kernel, nchunk=tb // GDN_CHUNK), "gdn", x, mix_norm_w, w,
                       [], cws + [hp, norm_w[None, :]], scratch, batch, tb)


def _mlp_kernel(x_ref, y0_ref, y1_ref, y2_ref, wo_ref, nw_ref, wu_ref, wd_ref, fw_ref, o_ref, xn_ref, acc_ref, *,
                final_norm):
    j = pl.program_id(1)

    @pl.when(j == 0)
    def _():
        x1 = x_ref[...]
        for i, y_ref in enumerate((y0_ref, y1_ref, y2_ref)):
            x1 = x1 + _dot(y_ref[...], wo_ref[pl.ds(i * SEG, SEG), :])
        xn_ref[...] = (_rms_rows(x1) * nw_ref[...]).astype(BF16)
        acc_ref[...] = x1

    h = jnp.maximum(_dot(xn_ref[...], wu_ref[...]), 0.0)
    acc_ref[...] += _dot((h * h).astype(BF16), wd_ref[...])

    @pl.when(j == pl.num_programs(1) - 1)
    def _():
        out = acc_ref[...]
        if final_norm:
            out = _rms_rows(out) * fw_ref[...]
        o_ref[...] = out


def _mlp(x, ys, w_out, norm_w, w_up, w_down, final_w, final_norm, tm=1024, tf=1024):
    t, d = x.shape
    ff = w_up.shape[1]
    tm = min(tm, t)
    rows = lambda width: pl.BlockSpec((tm, width), lambda i, j: (i, 0))
    full = lambda a: pl.BlockSpec(a.shape, lambda i, j: (0,) * a.ndim)
    return pl.pallas_call(
        functools.partial(_mlp_kernel, final_norm=final_norm),
        grid=(t // tm, ff // tf),
        in_specs=[rows(d), rows(SEG), rows(SEG), rows(SEG), full(w_out), full(norm_w),
                  pl.BlockSpec((d, tf), lambda i, j: (0, j)),
                  pl.BlockSpec((tf, d), lambda i, j: (j, 0)),
                  full(final_w)],
        out_specs=rows(d),
        out_shape=jax.ShapeDtypeStruct((t, d), F32),
        scratch_shapes=[pltpu.VMEM((tm, d), BF16), pltpu.VMEM((tm, d), F32)],
        compiler_params=_params("parallel", "arbitrary"),
        name="mlp",
    )(x, *ys, w_out, norm_w, w_up, w_down, final_w)


def _split_w_in(w):
    sizes = (SEG, SEG, SEG, SEG, SEG, 2 * SEG, SSD_HEADS, 3 * SEG, SEG, GDN_HEADS, GDN_HEADS)
    offs = np.concatenate([[0], np.cumsum(sizes)])
    piece = lambda i: w[:, offs[i]:offs[i + 1]]
    small = jnp.concatenate([piece(6), piece(9), piece(10)], axis=1)
    small = jnp.pad(small, ((0, 0), (0, LANE - small.shape[1])))
    slab = lambda idx: jnp.concatenate([piece(i) for i in idx], axis=1)
    w_ret = slab((0, 1, 2, 3))
    w_ssd = jnp.concatenate([slab((4, 5)), small], axis=1)
    w_gdn = jnp.concatenate([slab((7, 8)), small], axis=1)
    return w_ret.astype(BF16), w_ssd.astype(BF16), w_gdn.astype(BF16)


def kernel(x, positions, mix_norm_w, w_in, ret_norm_w, ssd_conv_w, ssd_conv_b, ssd_dt_bias, ssd_a_log, ssd_d,
           ssd_norm_w, gdn_conv_w, gdn_dt_bias, gdn_a_log, gdn_norm_w, w_out, mlp_norm_w, w_up, w_down,
           final_norm_w):
    batch, s, d = x.shape
    depth = w_in.shape[0]
    xf = x.reshape(batch * s, d)
    cosf, sinf = _rope_tables(positions)
    for l in range(depth):
        w_ret, w_ssd, w_gdn = _split_w_in(w_in[l])
        mnw = mix_norm_w[l][None, :]
        y_ret = _retention(xf, mnw, w_ret, cosf, sinf, ret_norm_w[l][None, :], batch)
        y_ssd = _ssd(xf, mnw, w_ssd, ssd_conv_w[l], ssd_conv_b[l], ssd_dt_bias[l], ssd_a_log[l], ssd_d[l],
                     ssd_norm_w[l][None, :], batch)
        y_gdn = _gdn(xf, mnw, w_gdn, gdn_conv_w[l], gdn_dt_bias[l], gdn_a_log[l], gdn_norm_w[l], batch)
        xf = _mlp(xf, (y_ret, y_ssd, y_gdn), w_out[l].astype(BF16), mlp_norm_w[l][None, :], w_up[l].astype(BF16),
                  w_down[l].astype(BF16), final_norm_w[None, :], final_norm=(l == depth - 1))
    return xf.reshape(batch, s, d)
```

```python
import functools
import math

import numpy as np
import jax
import jax.numpy as jnp
from jax import lax
from jax.experimental import pallas as pl
from jax.experimental.pallas import tpu as pltpu

F32 = jnp.float32
BF16 = jnp.bfloat16
HI = lax.Precision.HIGHEST

NORM_EPS = 1e-6
L2_EPS = 1e-6
ROPE_BASE = 10000.0

RET_HEADS = 4
RET_CHUNK = 128
SSD_HEADS = 8
SSD_GROUPS = 2
SSD_CHUNK = 128
GDN_HEADS = 4
GDN_CHUNK = 64
GDN_BASE = 8
CONV_K = 4

LANE = 128
SEG = 512
CONV_PAD = 8
VMEM_LIMIT = 56 * 1024 * 1024

SEG_RQ, SEG_RK, SEG_RV, SEG_RG, SEG_SZ, SEG_SX, SEG_SBC, SEG_GQ, SEG_GK, SEG_GV, SEG_GZ = range(11)
N_SEG = 11
SMALL_COL = N_SEG * SEG
D_PROJ = SMALL_COL + LANE
GDN_BETA_LANE = SSD_HEADS
GDN_DECAY_LANE = SSD_HEADS + GDN_HEADS


def _dot(a, b, precision=None):
    return jnp.dot(a, b, preferred_element_type=F32, precision=precision)


def _dot_nt(a, b, precision=None):
    return lax.dot_general(a, b, (((1,), (1,)), ((), ())), preferred_element_type=F32, precision=precision)


def _dot_tn(a, b, precision=None):
    return lax.dot_general(a, b, (((0,), (0,)), ((), ())), preferred_element_type=F32, precision=precision)


def _split3(x):
    hi = x.astype(BF16)
    r = x - hi.astype(F32)
    mid = r.astype(BF16)
    lo = (r - mid.astype(F32)).astype(BF16)
    return hi, mid, lo


def _cumsum_pair(tri3, tri3_t, x):
    parts = jnp.concatenate(_split3(x), axis=0)
    return _dot(tri3, parts), _dot_tn(parts, tri3_t)


def _expand_heads(x, e3_ref, nheads):
    hi, mid, lo = _split3(x)
    lane = lax.broadcasted_iota(jnp.int32, x.shape, 1)
    packed = jnp.where(lane < nheads, hi.astype(F32),
                       jnp.where(lane < 2 * nheads, pltpu.roll(mid.astype(F32), nheads, 1),
                                 jnp.where(lane < 3 * nheads, pltpu.roll(lo.astype(F32), 2 * nheads, 1), 0.0)))
    return _dot(packed.astype(BF16), e3_ref[...])


def _silu(t):
    return t * jax.nn.sigmoid(t)


def _softplus(t):
    return jnp.maximum(t, 0.0) + jnp.log1p(jnp.exp(-jnp.abs(t)))


def _rms_rows(t, eps=NORM_EPS):
    return t * lax.rsqrt(jnp.mean(t * t, axis=-1, keepdims=True) + eps)


def _params(*sem):
    return pltpu.CompilerParams(dimension_semantics=sem, vmem_limit_bytes=VMEM_LIMIT)


def _rope_kernel(pos_ref, freq_ref, cos_ref, sin_ref):
    ang = pos_ref[...].astype(F32) * freq_ref[...]
    lane = lax.broadcasted_iota(jnp.int32, ang.shape, 1)
    cos_ref[...] = jnp.cos(ang)
    sin_ref[...] = jnp.where(lane < LANE // 2, -jnp.sin(ang), jnp.sin(ang))


def _rope_tables(positions, tb=1024):
    t = positions.size
    half = LANE // 2
    inv_freq = ROPE_BASE ** (-jnp.arange(half, dtype=F32) / half)
    freq = jnp.concatenate([inv_freq, inv_freq])[None, :]
    tb = min(tb, t)
    return pl.pallas_call(
        _rope_kernel,
        grid=(t // tb,),
        in_specs=[pl.BlockSpec((tb, 1), lambda i: (i, 0)), pl.BlockSpec((1, LANE), lambda i: (0, 0))],
        out_specs=[pl.BlockSpec((tb, LANE), lambda i: (i, 0))] * 2,
        out_shape=[jax.ShapeDtypeStruct((t, LANE), F32)] * 2,
        compiler_params=_params("parallel"),
        name="rope_tables",
    )(positions.reshape(t, 1), freq)


def _project(x_ref, nw_ref, w_ref, proj_ref):
    xn = (_rms_rows(x_ref[...]) * nw_ref[...]).astype(BF16)
    n = w_ref.shape[1]
    for c0 in range(0, n, SEG):
        width = min(SEG, n - c0)
        proj_ref[:, pl.ds(c0, width)] = _dot(xn, w_ref[:, pl.ds(c0, width)])


def _project_slabs(x_ref, nw_ref, w_ref, proj_ref):
    xn = (_rms_rows(x_ref[...]) * nw_ref[...]).astype(BF16)
    n = w_ref.shape[1]

    def slab(c0, width):
        proj_ref[:, pl.ds(c0, width)] = _dot(xn, w_ref[:, pl.ds(c0, width)])

    return [functools.partial(slab, c0, min(SEG, n - c0)) for c0 in range(0, n, SEG)]


def _project_first(x0_ref, nw_ref, w_ref, proj_ref):
    @pl.when((pl.program_id(0) == 0) & (pl.program_id(1) == 0))
    def _():
        _project(x0_ref, nw_ref, w_ref, proj_ref)


def _seg_view(proj_ref, j, width=SEG):
    return proj_ref.at[:, pl.ds(j * SEG, width)]


def _ret_consts():
    c = RET_CHUNK
    h = RET_HEADS
    log_gamma = jnp.log1p(-jnp.exp2(-5.0 - jnp.arange(h, dtype=F32)))
    idx = jnp.arange(c, dtype=F32)
    rel = idx[:, None] - idx[None, :]
    causal = rel >= 0
    d_intra = jnp.where(causal, jnp.exp(log_gamma[:, None, None] * jnp.where(causal, rel, 0.0)), 0.0)
    zeta = jnp.exp(log_gamma[:, None] * (c - 1 - idx))
    xi = jnp.exp(log_gamma[:, None] * (idx + 1))
    cdec = jnp.exp(log_gamma * c)
    ones = jnp.ones((h, c, LANE), F32)
    tab = jnp.stack([zeta[:, :, None] * ones, xi[:, :, None] * ones, cdec[:, None, None] * ones], axis=1)
    return d_intra, tab


def _ret_kernel(x_ref, mnw_ref, w_ref, cos_ref, sin_ref, dm_ref, tab_ref, nw_ref, o_ref, proj_ref, st_ref, *,
                nchunk):
    c = RET_CHUNK

    @pl.when(pl.program_id(1) == 0)
    def _():
        st_ref[...] = jnp.zeros_like(st_ref)

    _project(x_ref, mnw_ref, w_ref, proj_ref)
    q_ref, k_ref, v_ref, g_ref = (_seg_view(proj_ref, j) for j in range(4))

    scale = LANE ** -0.5
    work = []
    for ci in range(nchunk):
        rows = pl.ds(ci * c, c)
        cos = cos_ref[rows, :]
        sin = sin_ref[rows, :]
        for h in range(RET_HEADS):
            cols = pl.ds(h * LANE, LANE)
            q = q_ref[rows, cols]
            k = k_ref[rows, cols]
            vb = v_ref[rows, cols].astype(BF16)
            q = (q * cos + pltpu.roll(q, LANE // 2, 1) * sin) * scale
            k = k * cos + pltpu.roll(k, LANE // 2, 1) * sin
            qb = q.astype(BF16)
            scores = _dot_nt(qb, k.astype(BF16)) * dm_ref[h]
            work.append(dict(qb=qb, y=_dot(scores.astype(BF16), vb),
                             kv=_dot_tn((k * tab_ref[h, 0]).astype(BF16), vb)))
    states = [st_ref[h] for h in range(RET_HEADS)]
    for ci in range(nchunk):
        rows = pl.ds(ci * c, c)
        for h in range(RET_HEADS):
            cols = pl.ds(h * LANE, LANE)
            wk = work[ci * RET_HEADS + h]
            y = wk["y"] + _dot(wk["qb"], states[h].astype(BF16)) * tab_ref[h, 1]
            states[h] = tab_ref[h, 2] * states[h] + wk["kv"]
            o_ref[rows, cols] = (_rms_rows(y) * nw_ref[:, cols] * _silu(g_ref[rows, cols])).astype(BF16)
    for h in range(RET_HEADS):
        st_ref[h] = states[h]


def _mixer_call(body, name, x, mix_norm_w, w, row_inputs, consts, scratch, batch, tb, pipelined):
    t, d = x.shape
    nb = t // batch // tb
    last = t // tb - 1
    rows = lambda width: pl.BlockSpec((tb, width), lambda b, i: (b * nb + i, 0))
    full = lambda a: pl.BlockSpec(a.shape, lambda b, i: (0,) * a.ndim)
    if pipelined:
        x_specs = [pl.BlockSpec((tb, d), lambda b, i: (0, 0)),
                   pl.BlockSpec((tb, d), lambda b, i: (jnp.minimum(b * nb + i + 1, last), 0))]
    else:
        x_specs = [rows(d)]
    in_specs = x_specs + [full(mix_norm_w), full(w)]
    in_specs += [rows(a.shape[1]) for a in row_inputs] + [full(a) for a in consts]
    consts = list(row_inputs) + list(consts)
    return pl.pallas_call(
        body,
        grid=(batch, nb),
        in_specs=in_specs,
        out_specs=rows(SEG),
        out_shape=jax.ShapeDtypeStruct((t, SEG), BF16),
        scratch_shapes=[pltpu.VMEM((tb, w.shape[1]), F32)] + scratch,
        compiler_params=_params("arbitrary", "arbitrary"),
        name=name,
    )(*([x] * len(x_specs)), mix_norm_w, w, *consts)


def _retention(x, mix_norm_w, w, cosf, sinf, norm_w, batch, tb=512):
    tb = min(tb, x.shape[0] // batch)
    d_intra, tab = _ret_consts()
    return _mixer_call(functools.partial(_ret_kernel, nchunk=tb // RET_CHUNK), "retention", x, mix_norm_w, w,
                       [cosf, sinf], [d_intra, tab, norm_w], [pltpu.VMEM((RET_HEADS, LANE, LANE), F32)], batch, tb,
                       pipelined=False)


def _causal_conv(x_ref, ext_ref, w_ref, first):
    tb = x_ref.shape[0]

    @pl.when(first)
    def _():
        ext_ref[pl.ds(0, CONV_PAD), :] = jnp.zeros((CONV_PAD, ext_ref.shape[1]), F32)

    x = x_ref[...]
    ext_ref[pl.ds(CONV_PAD, tb), :] = x
    acc = x * w_ref[CONV_K - 1:CONV_K, :]
    for j in range(CONV_K - 1):
        shift = CONV_K - 1 - j
        acc = acc + ext_ref[pl.ds(CONV_PAD - shift, tb), :] * w_ref[j:j + 1, :]
    ext_ref[pl.ds(0, CONV_PAD), :] = x[tb - CONV_PAD:, :]
    return acc


def _tri(n, strict=False, upper=False):
    r = lax.broadcasted_iota(jnp.int32, (n, n), 0)
    c = lax.broadcasted_iota(jnp.int32, (n, n), 1)
    if upper:
        r, c = c, r
    return (r > c) if strict else (r >= c)


def _ssd_kernel(x0_ref, xnext_ref, mnw_ref, w_ref, cwx_ref, cwbc_ref, cbx_ref, cbbc_ref, hp_ref, dfull_ref, e_ref,
                nw_ref, o_ref, proj_ref, extx_ref, extbc_ref, st_ref, *, nchunk):
    c = SSD_CHUNK
    nh = SSD_HEADS
    first = pl.program_id(1) == 0

    @pl.when(first)
    def _():
        st_ref[...] = jnp.zeros_like(st_ref)

    _project_first(x0_ref, mnw_ref, w_ref, proj_ref)
    z_ref, x_ref, bc_ref = (_seg_view(proj_ref, j) for j in range(3))
    sm_ref = _seg_view(proj_ref, 3, LANE)

    slabs = _project_slabs(xnext_ref, mnw_ref, w_ref, proj_ref)
    sm_all = sm_ref[...]
    xs_all = _silu(_causal_conv(x_ref, extx_ref, cwx_ref, first) + cbx_ref[...])
    slabs[1]()
    bc_all = _silu(_causal_conv(bc_ref, extbc_ref, cwbc_ref, first) + cbbc_ref[...])
    slabs[2]()
    gate_all = _silu(z_ref[...])
    slabs[0]()
    slabs[3]()
    dt_bias = hp_ref[0:1, :]
    a_neg = -jnp.exp(hp_ref[1:2, :])
    d_full = dfull_ref[...]
    causal = _tri(c)
    tri3 = jnp.concatenate([causal.astype(BF16)] * 3, axis=1)
    tri3_t = jnp.concatenate([_tri(c, upper=True).astype(BF16)] * 3, axis=0)
    lane = lax.broadcasted_iota(jnp.int32, (c, LANE), 1)
    gw = SEG // SSD_GROUPS
    ks = LANE
    states = [st_ref[g] for g in range(SSD_GROUPS)]

    for ci in range(nchunk):
        r0 = ci * c
        xs = xs_all[r0:r0 + c, :]
        bc = bc_all[r0:r0 + c, :]
        dt = _softplus(sm_all[r0:r0 + c, :] + dt_bias)
        acs_col, acs_row = _cumsum_pair(tri3, tri3_t, dt * a_neg)
        a_last = acs_col[c - 1:c, :]
        dt_full = _expand_heads(dt, e_ref, nh)
        ea_full = _expand_heads(jnp.exp(acs_col), e_ref, nh)
        te_full = _expand_heads(jnp.exp(a_last - acs_col), e_ref, nh)
        cd_full = ea_full[c - 1:c, :]
        xdt = xs * dt_full
        y_parts = []
        for g in range(SSD_GROUPS):
            bm = bc[:, g * ks:(g + 1) * ks].astype(BF16)
            cm = bc[:, SSD_GROUPS * ks + g * ks:SSD_GROUPS * ks + (g + 1) * ks].astype(BF16)
            cb = _dot_nt(cm, bm)
            gcols = slice(g * gw, (g + 1) * gw)
            prev = states[g]
            y_inter = _dot(cm, prev.astype(BF16)) * ea_full[:, gcols]
            states[g] = cd_full[:, gcols] * prev + _dot_tn(bm, (xdt[:, gcols] * te_full[:, gcols]).astype(BF16))
            for pair in range(gw // LANE):
                ms = []
                for sub in range(2):
                    hd = g * (nh // SSD_GROUPS) + pair * 2 + sub
                    seg = acs_col[:, hd:hd + 1] - acs_row[hd:hd + 1, :]
                    dec = jnp.where(causal, jnp.exp(jnp.where(causal, seg, 0.0)), 0.0)
                    ms.append((cb * dec).astype(BF16))
                xp = xdt[:, g * gw + pair * LANE:g * gw + (pair + 1) * LANE]
                lo = jnp.where(lane < LANE // 2, xp, 0.0).astype(BF16)
                hi = jnp.where(lane >= LANE // 2, xp, 0.0).astype(BF16)
                y_pair = _dot(jnp.concatenate(ms, axis=1), jnp.concatenate([lo, hi], axis=0))
                y_parts.append(y_pair + y_inter[:, pair * LANE:(pair + 1) * LANE])
        y = jnp.concatenate(y_parts, axis=1) + xs * d_full
        y = y * gate_all[r0:r0 + c, :]
        outs = [_rms_rows(y[:, g * gw:(g + 1) * gw]) for g in range(SSD_GROUPS)]
        o_ref[pl.ds(r0, c), :] = (jnp.concatenate(outs, axis=1) * nw_ref[...]).astype(BF16)
    for g in range(SSD_GROUPS):
        st_ref[g] = states[g]


def _pad_rows(a, rows=8, cols=LANE):
    out = jnp.zeros((rows, cols), F32)
    return out.at[:a.shape[0], :a.shape[1]].set(a.astype(F32))


def _ssd(x, mix_norm_w, w, conv_w, conv_b, dt_bias, a_log, d_skip, norm_w, batch, tb=512):
    tb = min(tb, x.shape[0] // batch)
    hp = _pad_rows(jnp.stack([dt_bias, a_log]))
    hw = SEG // SSD_HEADS
    d_full = jnp.repeat(d_skip.astype(F32), hw)[None, :]
    expand = jnp.tile(jnp.repeat(jnp.eye(SSD_HEADS, dtype=BF16), hw, axis=1), (3, 1))
    expand = jnp.pad(expand, ((0, LANE - 3 * SSD_HEADS), (0, 0)))
    cwx, cwbc = conv_w[:, :SEG], conv_w[:, SEG:]
    cbx, cbbc = conv_b[None, :SEG], conv_b[None, SEG:]
    consts = [cwx, cwbc, cbx, cbbc, hp, d_full, expand, norm_w]
    scratch = [pltpu.VMEM((tb + CONV_PAD, SEG), F32), pltpu.VMEM((tb + CONV_PAD, SEG), F32),
               pltpu.VMEM((SSD_GROUPS, LANE, SEG // SSD_GROUPS), F32)]
    return _mixer_call(functools.partial(_ssd_kernel, nchunk=tb // SSD_CHUNK), "ssd", x, mix_norm_w, w,
                       [], consts, scratch, batch, tb, pipelined=True)


def _gdn_kernel(x0_ref, xnext_ref, mnw_ref, w_ref, cwq_ref, cwk_ref, cwv_ref, hp_ref, nw_ref,
                o_ref, proj_ref, extq_ref, extk_ref, extv_ref, st_ref, *, nchunk):
    c = GDN_CHUNK
    nh = GDN_HEADS
    first = pl.program_id(1) == 0

    @pl.when(first)
    def _():
        st_ref[...] = jnp.zeros_like(st_ref)

    _project_first(x0_ref, mnw_ref, w_ref, proj_ref)
    q_ref, k_ref, v_ref, z_ref = (_seg_view(proj_ref, j) for j in range(4))
    sm_ref = _seg_view(proj_ref, 4, LANE)

    slabs = _project_slabs(xnext_ref, mnw_ref, w_ref, proj_ref)
    sm = sm_ref[...]
    q_all = _silu(_causal_conv(q_ref, extq_ref, cwq_ref, first))
    slabs[0]()
    k_all = _silu(_causal_conv(k_ref, extk_ref, cwk_ref, first))
    slabs[1]()
    v_all = _silu(_causal_conv(v_ref, extv_ref, cwv_ref, first))
    slabs[2]()
    gate_all = _silu(z_ref[...])
    slabs[3]()
    slabs[4]()
    dt_bias = hp_ref[0:1, :]
    a_neg = -jnp.exp(hp_ref[1:2, :])
    incl = _tri(c)
    strict = _tri(c, strict=True)
    scale = LANE ** -0.5
    beta = jax.nn.sigmoid(sm)
    g = a_neg * _softplus(sm + dt_bias)
    tri3 = jnp.concatenate([incl.astype(BF16)] * 3, axis=1)
    tri3_t = jnp.concatenate([_tri(c, upper=True).astype(BF16)] * 3, axis=0)

    work = []
    for ci in range(nchunk):
        r0 = ci * c
        gcs_col, gcs_row = _cumsum_pair(tri3, tri3_t, g[r0:r0 + c, :])
        g_last = gcs_col[c - 1:c, :]
        eg_all = jnp.exp(gcs_col)
        ekd_all = jnp.exp(g_last - gcs_col)
        cdec_all = jnp.exp(g_last)
        for h in range(nh):
            cols = slice(h * LANE, (h + 1) * LANE)
            dl = GDN_DECAY_LANE + h
            eg = eg_all[:, dl:dl + 1]
            q = q_all[r0:r0 + c, cols]
            k = k_all[r0:r0 + c, cols]
            v = v_all[r0:r0 + c, cols]
            q = q * lax.rsqrt(jnp.sum(q * q, axis=-1, keepdims=True) + L2_EPS) * scale
            k = k * lax.rsqrt(jnp.sum(k * k, axis=-1, keepdims=True) + L2_EPS)
            bh = beta[r0:r0 + c, GDN_BETA_LANE + h:GDN_BETA_LANE + h + 1]
            kb = k * bh
            diff = gcs_col[:, dl:dl + 1] - gcs_row[dl:dl + 1, :]
            decay = jnp.where(incl, jnp.exp(jnp.where(incl, diff, 0.0)), 0.0)
            kq = _dot_nt(jnp.concatenate([kb, q], axis=0).astype(BF16), k.astype(BF16))
            work.append(dict(
                low=jnp.where(strict, kq[:c] * decay, 0.0),
                attn=(kq[c:] * decay).astype(BF16),
                sol=jnp.concatenate([v * bh, kb * eg], axis=1),
                qd=(q * eg).astype(BF16),
                kd=(k * ekd_all[:, dl:dl + 1]).astype(BF16),
                cdec=cdec_all[:, dl:dl + 1]))

    rb = lax.broadcasted_iota(jnp.int32, (c, c), 0)
    cb = lax.broadcasted_iota(jnp.int32, (c, c), 1)
    same = lambda n: (rb // n) == (cb // n)
    eye = (rb == cb).astype(F32)
    for wk in work:
        wk["n"] = jnp.where(same(GDN_BASE), -wk["low"], 0.0).astype(BF16)
    for wk in work:
        wk["n2"] = _dot(wk["n"], wk["n"]).astype(BF16)
        wk["t"] = eye + wk["n"].astype(F32)
    for wk in work:
        wk["t"] = wk["t"] + _dot(wk["t"].astype(BF16), wk["n2"])
        wk["n4"] = _dot(wk["n2"], wk["n2"]).astype(BF16)
    for wk in work:
        wk["t"] = wk["t"] + _dot(wk["t"].astype(BF16), wk["n4"])
    n = GDN_BASE
    while n < c:
        off = same(2 * n) & jnp.logical_not(same(n))
        for wk in work:
            tb16 = wk["t"].astype(BF16)
            wk["tc"] = (_dot(tb16, jnp.where(off, wk["low"], 0.0).astype(BF16)).astype(BF16), tb16)
        for wk in work:
            tc, tb16 = wk["tc"]
            wk["t"] = wk["t"] - _dot(tc, tb16)
        n *= 2
    for wk in work:
        wk["sol"] = _dot(wk["t"].astype(BF16), wk["sol"].astype(BF16))

    states = [st_ref[h] for h in range(nh)]
    for ci in range(nchunk):
        rows = pl.ds(ci * c, c)
        for h in range(nh):
            cols = slice(h * LANE, (h + 1) * LANE)
            wk = work[ci * nh + h]
            st = states[h]
            ws = _dot(jnp.concatenate([wk["sol"][:, LANE:].astype(BF16), wk["qd"]], axis=0), st.astype(BF16))
            v_new = (wk["sol"][:, :LANE] - ws[:c]).astype(BF16)
            o = ws[c:] + _dot(wk["attn"], v_new)
            states[h] = st * wk["cdec"] + _dot_tn(wk["kd"], v_new)
            o_ref[rows, cols] = (_rms_rows(o) * nw_ref[...] * gate_all[ci * c:(ci + 1) * c, cols]).astype(BF16)
    for h in range(nh):
        st_ref[h] = states[h]


def _gdn(x, mix_norm_w, w, conv_w, dt_bias, a_log, norm_w, batch, tb=512):
    tb = min(tb, x.shape[0] // batch)
    hp = jnp.zeros((8, LANE), F32).at[0:2, GDN_DECAY_LANE:GDN_DECAY_LANE + GDN_HEADS].set(
        jnp.stack([dt_bias, a_log]).astype(F32))
    cws = [conv_w[:, i * SEG:(i + 1) * SEG] for i in range(3)]
    scratch = [pltpu.VMEM((tb + CONV_PAD, SEG), F32)] * 3 + [pltpu.VMEM((GDN_HEADS, LANE, LANE), F32)]
    return _mixer_call(functools.partial(_gdn_kernel, nchunk=tb // GDN_CHUNK), "gdn", x, mix_norm_w, w,
                       [], cws + [hp, norm_w[None, :]], scratch, batch, tb, pipelined=True)


def _mlp_kernel(x_ref, y0_ref, y1_ref, y2_ref, wo_ref, nw_ref, wu_ref, wd_ref, fw_ref, o_ref, xn_ref, acc_ref, *,
                final_norm):
    j = pl.program_id(1)

    @pl.when(j == 0)
    def _():
        x1 = x_ref[...]
        for i, y_ref in enumerate((y0_ref, y1_ref, y2_ref)):
            x1 = x1 + _dot(y_ref[...], wo_ref[pl.ds(i * SEG, SEG), :])
        xn_ref[...] = (_rms_rows(x1) * nw_ref[...]).astype(BF16)
        acc_ref[...] = x1

    h = jnp.maximum(_dot(xn_ref[...], wu_ref[...]), 0.0)
    acc_ref[...] += _dot((h * h).astype(BF16), wd_ref[...])

    @pl.when(j == pl.num_programs(1) - 1)
    def _():
        out = acc_ref[...]
        if final_norm:
            out = _rms_rows(out) * fw_ref[...]
        o_ref[...] = out


def _mlp(x, ys, w_out, norm_w, w_up, w_down, final_w, final_norm, tm=1024, tf=1024):
    t, d = x.shape
    ff = w_up.shape[1]
    tm = min(tm, t)
    rows = lambda width: pl.BlockSpec((tm, width), lambda i, j: (i, 0))
    full = lambda a: pl.BlockSpec(a.shape, lambda i, j: (0,) * a.ndim)
    return pl.pallas_call(
        functools.partial(_mlp_kernel, final_norm=final_norm),
        grid=(t // tm, ff // tf),
        in_specs=[rows(d), rows(SEG), rows(SEG), rows(SEG), full(w_out), full(norm_w),
                  pl.BlockSpec((d, tf), lambda i, j: (0, j)),
                  pl.BlockSpec((tf, d), lambda i, j: (j, 0)),
                  full(final_w)],
        out_specs=rows(d),
        out_shape=jax.ShapeDtypeStruct((t, d), F32),
        scratch_shapes=[pltpu.VMEM((tm, d), BF16), pltpu.VMEM((tm, d), F32)],
        compiler_params=_params("parallel", "arbitrary"),
        name="mlp",
    )(x, *ys, w_out, norm_w, w_up, w_down, final_w)


def _split_w_in(w):
    sizes = (SEG, SEG, SEG, SEG, SEG, 2 * SEG, SSD_HEADS, 3 * SEG, SEG, GDN_HEADS, GDN_HEADS)
    offs = np.concatenate([[0], np.cumsum(sizes)])
    piece = lambda i: w[:, offs[i]:offs[i + 1]]
    small = jnp.concatenate([piece(6), piece(9), piece(10)], axis=1)
    small = jnp.pad(small, ((0, 0), (0, LANE - small.shape[1])))
    slab = lambda idx: jnp.concatenate([piece(i) for i in idx], axis=1)
    w_ret = slab((0, 1, 2, 3))
    w_ssd = jnp.concatenate([slab((4, 5)), small], axis=1)
    w_gdn = jnp.concatenate([slab((7, 8)), small], axis=1)
    return w_ret.astype(BF16), w_ssd.astype(BF16), w_gdn.astype(BF16)


def kernel(x, positions, mix_norm_w, w_in, ret_norm_w, ssd_conv_w, ssd_conv_b, ssd_dt_bias, ssd_a_log, ssd_d,
           ssd_norm_w, gdn_conv_w, gdn_dt_bias, gdn_a_log, gdn_norm_w, w_out, mlp_norm_w, w_up, w_down,
           final_norm_w):
    batch, s, d = x.shape
    depth = w_in.shape[0]
    xf = x.reshape(batch * s, d)
    cosf, sinf = _rope_tables(positions)
    for l in range(depth):
        w_ret, w_ssd, w_gdn = _split_w_in(w_in[l])
        mnw = mix_norm_w[l][None, :]
        y_ret = _retention(xf, mnw, w_ret, cosf, sinf, ret_norm_w[l][None, :], batch)
        y_ssd = _ssd(xf, mnw, w_ssd, ssd_conv_w[l], ssd_conv_b[l], ssd_dt_bias[l], ssd_a_log[l], ssd_d[l],
                     ssd_norm_w[l][None, :], batch)
        y_gdn = _gdn(xf, mnw, w_gdn, gdn_conv_w[l], gdn_dt_bias[l], gdn_a_log[l], gdn_norm_w[l], batch)
        xf = _mlp(xf, (y_ret, y_ssd, y_gdn), w_out[l].astype(BF16), mlp_norm_w[l][None, :], w_up[l].astype(BF16),
                  w_down[l].astype(BF16), final_norm_w[None, :], final_norm=(l == depth - 1))
    return xf.reshape(batch, s, d)
```

```python
import functools
import math

import numpy as np
import jax
import jax.numpy as jnp
from jax import lax
from jax.experimental import pallas as pl
from jax.experimental.pallas import tpu as pltpu

F32 = jnp.float32
BF16 = jnp.bfloat16
HI = lax.Precision.HIGHEST

NORM_EPS = 1e-6
L2_EPS = 1e-6
ROPE_BASE = 10000.0

RET_HEADS = 4
RET_CHUNK = 128
SSD_HEADS = 8
SSD_GROUPS = 2
SSD_CHUNK = 128
GDN_HEADS = 4
GDN_CHUNK = 64
GDN_BASE = 8
CONV_K = 4

LANE = 128
SEG = 512
CONV_PAD = 8
VMEM_LIMIT = 56 * 1024 * 1024

SEG_RQ, SEG_RK, SEG_RV, SEG_RG, SEG_SZ, SEG_SX, SEG_SBC, SEG_GQ, SEG_GK, SEG_GV, SEG_GZ = range(11)
N_SEG = 11
SMALL_COL = N_SEG * SEG
D_PROJ = SMALL_COL + LANE
GDN_BETA_LANE = SSD_HEADS
GDN_DECAY_LANE = SSD_HEADS + GDN_HEADS


def _dot(a, b, precision=None):
    return jnp.dot(a, b, preferred_element_type=F32, precision=precision)


def _dot_nt(a, b, precision=None):
    return lax.dot_general(a, b, (((1,), (1,)), ((), ())), preferred_element_type=F32, precision=precision)


def _dot_tn(a, b, precision=None):
    return lax.dot_general(a, b, (((0,), (0,)), ((), ())), preferred_element_type=F32, precision=precision)


def _split3(x):
    hi = x.astype(BF16)
    r = x - hi.astype(F32)
    mid = r.astype(BF16)
    lo = (r - mid.astype(F32)).astype(BF16)
    return hi, mid, lo


def _cumsum_pair(tri3, tri3_t, x):
    parts = jnp.concatenate(_split3(x), axis=0)
    return _dot(tri3, parts), _dot_tn(parts, tri3_t)


def _expand_heads(x, e3_ref, nheads):
    hi, mid, lo = _split3(x)
    lane = lax.broadcasted_iota(jnp.int32, x.shape, 1)
    packed = jnp.where(lane < nheads, hi.astype(F32),
                       jnp.where(lane < 2 * nheads, pltpu.roll(mid.astype(F32), nheads, 1),
                                 jnp.where(lane < 3 * nheads, pltpu.roll(lo.astype(F32), 2 * nheads, 1), 0.0)))
    return _dot(packed.astype(BF16), e3_ref[...])


def _silu(t):
    return t * jax.nn.sigmoid(t)


def _softplus(t):
    return jnp.maximum(t, 0.0) + jnp.log1p(jnp.exp(-jnp.abs(t)))


def _rms_rows(t, eps=NORM_EPS):
    return t * lax.rsqrt(jnp.mean(t * t, axis=-1, keepdims=True) + eps)


def _params(*sem):
    return pltpu.CompilerParams(dimension_semantics=sem, vmem_limit_bytes=VMEM_LIMIT)


def _rope_kernel(pos_ref, freq_ref, cos_ref, sin_ref):
    ang = pos_ref[...].astype(F32) * freq_ref[...]
    lane = lax.broadcasted_iota(jnp.int32, ang.shape, 1)
    cos_ref[...] = jnp.cos(ang)
    sin_ref[...] = jnp.where(lane < LANE // 2, -jnp.sin(ang), jnp.sin(ang))


def _rope_tables(positions, tb=1024):
    t = positions.size
    half = LANE // 2
    inv_freq = ROPE_BASE ** (-jnp.arange(half, dtype=F32) / half)
    freq = jnp.concatenate([inv_freq, inv_freq])[None, :]
    tb = min(tb, t)
    return pl.pallas_call(
        _rope_kernel,
        grid=(t // tb,),
        in_specs=[pl.BlockSpec((tb, 1), lambda i: (i, 0)), pl.BlockSpec((1, LANE), lambda i: (0, 0))],
        out_specs=[pl.BlockSpec((tb, LANE), lambda i: (i, 0))] * 2,
        out_shape=[jax.ShapeDtypeStruct((t, LANE), F32)] * 2,
        compiler_params=_params("parallel"),
        name="rope_tables",
    )(positions.reshape(t, 1), freq)


def _project(x_ref, nw_ref, w_ref, proj_ref):
    xn = (_rms_rows(x_ref[...]) * nw_ref[...]).astype(BF16)
    n = w_ref.shape[1]
    for c0 in range(0, n, SEG):
        width = min(SEG, n - c0)
        proj_ref[:, pl.ds(c0, width)] = _dot(xn, w_ref[:, pl.ds(c0, width)])


def _project_slabs(x_ref, nw_ref, w_ref, proj_ref):
    xn = (_rms_rows(x_ref[...]) * nw_ref[...]).astype(BF16)
    n = w_ref.shape[1]

    def slab(c0, width):
        proj_ref[:, pl.ds(c0, width)] = _dot(xn, w_ref[:, pl.ds(c0, width)])

    return [functools.partial(slab, c0, min(SEG, n - c0)) for c0 in range(0, n, SEG)]


def _project_first(x0_ref, nw_ref, w_ref, proj_ref):
    @pl.when((pl.program_id(0) == 0) & (pl.program_id(1) == 0))
    def _():
        _project(x0_ref, nw_ref, w_ref, proj_ref)


def _seg_view(proj_ref, j, width=SEG):
    return proj_ref.at[:, pl.ds(j * SEG, width)]


def _ret_consts():
    c = RET_CHUNK
    h = RET_HEADS
    log_gamma = jnp.log1p(-jnp.exp2(-5.0 - jnp.arange(h, dtype=F32)))
    idx = jnp.arange(c, dtype=F32)
    rel = idx[:, None] - idx[None, :]
    causal = rel >= 0
    d_intra = jnp.where(causal, jnp.exp(log_gamma[:, None, None] * jnp.where(causal, rel, 0.0)), 0.0)
    zeta = jnp.exp(log_gamma[:, None] * (c - 1 - idx))
    xi = jnp.exp(log_gamma[:, None] * (idx + 1))
    cdec = jnp.exp(log_gamma * c)
    ones = jnp.ones((h, c, LANE), F32)
    tab = jnp.stack([zeta[:, :, None] * ones, xi[:, :, None] * ones, cdec[:, None, None] * ones], axis=1)
    return d_intra, tab


def _ret_kernel(x_ref, mnw_ref, w_ref, cos_ref, sin_ref, dm_ref, tab_ref, nw_ref, o_ref, proj_ref, st_ref, *,
                nchunk):
    c = RET_CHUNK

    @pl.when(pl.program_id(1) == 0)
    def _():
        st_ref[...] = jnp.zeros_like(st_ref)

    _project(x_ref, mnw_ref, w_ref, proj_ref)
    q_ref, k_ref, v_ref, g_ref = (_seg_view(proj_ref, j) for j in range(4))

    scale = LANE ** -0.5
    work = []
    for ci in range(nchunk):
        rows = pl.ds(ci * c, c)
        cos = cos_ref[rows, :]
        sin = sin_ref[rows, :]
        for h in range(RET_HEADS):
            cols = pl.ds(h * LANE, LANE)
            q = q_ref[rows, cols]
            k = k_ref[rows, cols]
            vb = v_ref[rows, cols].astype(BF16)
            q = (q * cos + pltpu.roll(q, LANE // 2, 1) * sin) * scale
            k = k * cos + pltpu.roll(k, LANE // 2, 1) * sin
            qb = q.astype(BF16)
            scores = _dot_nt(qb, k.astype(BF16)) * dm_ref[h]
            work.append(dict(qb=qb, y=_dot(scores.astype(BF16), vb),
                             kv=_dot_tn((k * tab_ref[h, 0]).astype(BF16), vb)))
    states = [st_ref[h] for h in range(RET_HEADS)]
    for ci in range(nchunk):
        rows = pl.ds(ci * c, c)
        for h in range(RET_HEADS):
            cols = pl.ds(h * LANE, LANE)
            wk = work[ci * RET_HEADS + h]
            y = wk["y"] + _dot(wk["qb"], states[h].astype(BF16)) * tab_ref[h, 1]
            states[h] = tab_ref[h, 2] * states[h] + wk["kv"]
            o_ref[rows, cols] = (_rms_rows(y) * nw_ref[:, cols] * _silu(g_ref[rows, cols])).astype(BF16)
    for h in range(RET_HEADS):
        st_ref[h] = states[h]


def _mixer_call(body, name, x, mix_norm_w, w, row_inputs, consts, scratch, batch, tb, pipelined):
    t, d = x.shape
    nb = t // batch // tb
    last = t // tb - 1
    rows = lambda width: pl.BlockSpec((tb, width), lambda b, i: (b * nb + i, 0))
    full = lambda a: pl.BlockSpec(a.shape, lambda b, i: (0,) * a.ndim)
    if pipelined:
        x_specs = [pl.BlockSpec((tb, d), lambda b, i: (0, 0)),
                   pl.BlockSpec((tb, d), lambda b, i: (jnp.minimum(b * nb + i + 1, last), 0))]
    else:
        x_specs = [rows(d)]
    in_specs = x_specs + [full(mix_norm_w), full(w)]
    in_specs += [rows(a.shape[1]) for a in row_inputs] + [full(a) for a in consts]
    consts = list(row_inputs) + list(consts)
    return pl.pallas_call(
        body,
        grid=(batch, nb),
        in_specs=in_specs,
        out_specs=rows(SEG),
        out_shape=jax.ShapeDtypeStruct((t, SEG), BF16),
        scratch_shapes=[pltpu.VMEM((tb, w.shape[1]), F32)] + scratch,
        compiler_params=_params("arbitrary", "arbitrary"),
        name=name,
    )(*([x] * len(x_specs)), mix_norm_w, w, *consts)


def _retention(x, mix_norm_w, w, cosf, sinf, norm_w, batch, tb=512):
    tb = min(tb, x.shape[0] // batch)
    d_intra, tab = _ret_consts()
    return _mixer_call(functools.partial(_ret_kernel, nchunk=tb // RET_CHUNK), "retention", x, mix_norm_w, w,
                       [cosf, sinf], [d_intra, tab, norm_w], [pltpu.VMEM((RET_HEADS, LANE, LANE), F32)], batch, tb,
                       pipelined=False)


def _causal_conv(x_ref, ext_ref, w_ref, first):
    tb = x_ref.shape[0]

    @pl.when(first)
    def _():
        ext_ref[pl.ds(0, CONV_PAD), :] = jnp.zeros((CONV_PAD, ext_ref.shape[1]), F32)

    x = x_ref[...]
    ext_ref[pl.ds(CONV_PAD, tb), :] = x
    acc = x * w_ref[CONV_K - 1:CONV_K, :]
    for j in range(CONV_K - 1):
        shift = CONV_K - 1 - j
        acc = acc + ext_ref[pl.ds(CONV_PAD - shift, tb), :] * w_ref[j:j + 1, :]
    ext_ref[pl.ds(0, CONV_PAD), :] = x[tb - CONV_PAD:, :]
    return acc


def _tri(n, strict=False, upper=False):
    r = lax.broadcasted_iota(jnp.int32, (n, n), 0)
    c = lax.broadcasted_iota(jnp.int32, (n, n), 1)
    if upper:
        r, c = c, r
    return (r > c) if strict else (r >= c)


def _ssd_kernel(x0_ref, xnext_ref, mnw_ref, w_ref, cwx_ref, cwbc_ref, cbx_ref, cbbc_ref, hp_ref, dfull_ref, e_ref,
                nw_ref, o_ref, proj_ref, extx_ref, extbc_ref, st_ref, *, nchunk):
    c = SSD_CHUNK
    nh = SSD_HEADS
    first = pl.program_id(1) == 0

    @pl.when(first)
    def _():
        st_ref[...] = jnp.zeros_like(st_ref)

    _project_first(x0_ref, mnw_ref, w_ref, proj_ref)
    z_ref, x_ref, bc_ref = (_seg_view(proj_ref, j) for j in range(3))
    sm_ref = _seg_view(proj_ref, 3, LANE)

    slabs = _project_slabs(xnext_ref, mnw_ref, w_ref, proj_ref)
    sm_all = sm_ref[...]
    xs_all = _silu(_causal_conv(x_ref, extx_ref, cwx_ref, first) + cbx_ref[...])
    slabs[1]()
    bc_all = _silu(_causal_conv(bc_ref, extbc_ref, cwbc_ref, first) + cbbc_ref[...])
    slabs[2]()
    gate_all = _silu(z_ref[...])
    slabs[0]()
    slabs[3]()
    dt_bias = hp_ref[0:1, :]
    a_neg = -jnp.exp(hp_ref[1:2, :])
    d_full = dfull_ref[...]
    causal = _tri(c)
    tri3 = jnp.concatenate([causal.astype(BF16)] * 3, axis=1)
    tri3_t = jnp.concatenate([_tri(c, upper=True).astype(BF16)] * 3, axis=0)
    lane = lax.broadcasted_iota(jnp.int32, (c, LANE), 1)
    gw = SEG // SSD_GROUPS
    ks = LANE
    states = [st_ref[g] for g in range(SSD_GROUPS)]

    for ci in range(nchunk):
        r0 = ci * c
        xs = xs_all[r0:r0 + c, :]
        bc = bc_all[r0:r0 + c, :]
        dt = _softplus(sm_all[r0:r0 + c, :] + dt_bias)
        acs_col, acs_row = _cumsum_pair(tri3, tri3_t, dt * a_neg)
        a_last = acs_col[c - 1:c, :]
        dt_full = _expand_heads(dt, e_ref, nh)
        ea_full = _expand_heads(jnp.exp(acs_col), e_ref, nh)
        te_full = _expand_heads(jnp.exp(a_last - acs_col), e_ref, nh)
        cd_full = ea_full[c - 1:c, :]
        xdt = xs * dt_full
        y_parts = []
        for g in range(SSD_GROUPS):
            bm = bc[:, g * ks:(g + 1) * ks].astype(BF16)
            cm = bc[:, SSD_GROUPS * ks + g * ks:SSD_GROUPS * ks + (g + 1) * ks].astype(BF16)
            cb = _dot_nt(cm, bm)
            gcols = slice(g * gw, (g + 1) * gw)
            prev = states[g]
            y_inter = _dot(cm, prev.astype(BF16)) * ea_full[:, gcols]
            states[g] = cd_full[:, gcols] * prev + _dot_tn(bm, (xdt[:, gcols] * te_full[:, gcols]).astype(BF16))
            for pair in range(gw // LANE):
                ms = []
                for sub in range(2):
                    hd = g * (nh // SSD_GROUPS) + pair * 2 + sub
                    seg = acs_col[:, hd:hd + 1] - acs_row[hd:hd + 1, :]
                    dec = jnp.where(causal, jnp.exp(jnp.where(causal, seg, 0.0)), 0.0)
                    ms.append((cb * dec).astype(BF16))
                xp = xdt[:, g * gw + pair * LANE:g * gw + (pair + 1) * LANE]
                lo = jnp.where(lane < LANE // 2, xp, 0.0).astype(BF16)
                hi = jnp.where(lane >= LANE // 2, xp, 0.0).astype(BF16)
                y_pair = _dot(jnp.concatenate(ms, axis=1), jnp.concatenate([lo, hi], axis=0))
                y_parts.append(y_pair + y_inter[:, pair * LANE:(pair + 1) * LANE])
        y = jnp.concatenate(y_parts, axis=1) + xs * d_full
        y = y * gate_all[r0:r0 + c, :]
        outs = [_rms_rows(y[:, g * gw:(g + 1) * gw]) for g in range(SSD_GROUPS)]
        o_ref[pl.ds(r0, c), :] = (jnp.concatenate(outs, axis=1) * nw_ref[...]).astype(BF16)
    for g in range(SSD_GROUPS):
        st_ref[g] = states[g]


def _pad_rows(a, rows=8, cols=LANE):
    out = jnp.zeros((rows, cols), F32)
    return out.at[:a.shape[0], :a.shape[1]].set(a.astype(F32))


def _ssd(x, mix_norm_w, w, conv_w, conv_b, dt_bias, a_log, d_skip, norm_w, batch, tb=512):
    tb = min(tb, x.shape[0] // batch)
    hp = _pad_rows(jnp.stack([dt_bias, a_log]))
    hw = SEG // SSD_HEADS
    d_full = jnp.repeat(d_skip.astype(F32), hw)[None, :]
    expand = jnp.tile(jnp.repeat(jnp.eye(SSD_HEADS, dtype=BF16), hw, axis=1), (3, 1))
    expand = jnp.pad(expand, ((0, LANE - 3 * SSD_HEADS), (0, 0)))
    cwx, cwbc = conv_w[:, :SEG], conv_w[:, SEG:]
    cbx, cbbc = conv_b[None, :SEG], conv_b[None, SEG:]
    consts = [cwx, cwbc, cbx, cbbc, hp, d_full, expand, norm_w]
    scratch = [pltpu.VMEM((tb + CONV_PAD, SEG), F32), pltpu.VMEM((tb + CONV_PAD, SEG), F32),
               pltpu.VMEM((SSD_GROUPS, LANE, SEG // SSD_GROUPS), F32)]
    return _mixer_call(functools.partial(_ssd_kernel, nchunk=tb // SSD_CHUNK), "ssd", x, mix_norm_w, w,
                       [], consts, scratch, batch, tb, pipelined=True)


def _gdn_kernel(x0_ref, xnext_ref, mnw_ref, w_ref, cwq_ref, cwk_ref, cwv_ref, hp_ref, nw_ref,
                o_ref, proj_ref, extq_ref, extk_ref, extv_ref, st_ref, *, nchunk):
    c = GDN_CHUNK
    nh = GDN_HEADS
    first = pl.program_id(1) == 0

    @pl.when(first)
    def _():
        st_ref[...] = jnp.zeros_like(st_ref)

    _project_first(x0_ref, mnw_ref, w_ref, proj_ref)
    q_ref, k_ref, v_ref, z_ref = (_seg_view(proj_ref, j) for j in range(4))
    sm_ref = _seg_view(proj_ref, 4, LANE)

    slabs = _project_slabs(xnext_ref, mnw_ref, w_ref, proj_ref)
    sm = sm_ref[...]
    q_all = _silu(_causal_conv(q_ref, extq_ref, cwq_ref, first))
    slabs[0]()
    k_all = _silu(_causal_conv(k_ref, extk_ref, cwk_ref, first))
    slabs[1]()
    v_all = _silu(_causal_conv(v_ref, extv_ref, cwv_ref, first))
    slabs[2]()
    gate_all = _silu(z_ref[...])
    slabs[3]()
    slabs[4]()
    dt_bias = hp_ref[0:1, :]
    a_neg = -jnp.exp(hp_ref[1:2, :])
    scale = LANE ** -0.5
    pw = nh * c
    beta = jax.nn.sigmoid(sm)
    g = a_neg * _softplus(sm + dt_bias)
    tri3 = jnp.concatenate([_tri(c).astype(BF16)] * 3, axis=1)
    tri3_t = jnp.concatenate([_tri(c, upper=True).astype(BF16)] * 3, axis=0)
    tri3_t2 = jnp.concatenate([tri3_t] * (LANE // c), axis=1)

    row_p = lax.broadcasted_iota(jnp.int32, (c, pw), 0)
    lane_p = lax.broadcasted_iota(jnp.int32, (c, pw), 1)
    col_p = jnp.bitwise_and(lane_p, c - 1)
    incl_p = row_p >= col_p
    strict_p = row_p > col_p
    eye_p = (row_p == col_p).astype(F32)
    same_p = lambda n: (row_p // n) == (col_p // n)
    lane_1 = lax.broadcasted_iota(jnp.int32, (c, LANE), 1)
    head_p = [((lane_p // c) == h).astype(BF16) for h in range(nh)]
    lane_s = lax.broadcasted_iota(jnp.int32, (c, SEG), 1)
    head_s = [((lane_s // LANE) == h).astype(BF16) for h in range(nh)]

    def block_rows(y, masks):
        return jnp.concatenate([y * m for m in masks], axis=0)

    def bcast_nat(x, lane0):
        r = x.shape[0]
        return jnp.concatenate([jnp.broadcast_to(x[:, lane0 + h:lane0 + h + 1], (r, LANE)) for h in range(nh)], axis=1)

    def bcast_packed(x, lane0):
        tiles = []
        for t in range(pw // LANE):
            per = LANE // c
            tile = jnp.broadcast_to(x[:, lane0 + t * per:lane0 + t * per + 1], (c, LANE))
            for j in range(1, per):
                tile = jnp.where(lane_1 < j * c, tile,
                                 jnp.broadcast_to(x[:, lane0 + t * per + j:lane0 + t * per + j + 1], (c, LANE)))
            tiles.append(tile)
        return jnp.concatenate(tiles, axis=1)

    def l2n(t):
        parts = []
        for h in range(nh):
            th = t[:, h * LANE:(h + 1) * LANE]
            parts.append(th * lax.rsqrt(jnp.sum(th * th, axis=-1, keepdims=True) + L2_EPS))
        return jnp.concatenate(parts, axis=1)

    qn_all = l2n(q_all) * scale
    kn_all = l2n(k_all)

    work = []
    for ci in range(nchunk):
        r0 = ci * c
        gcs_col, gcs_row = _cumsum_pair(tri3, tri3_t2, g[r0:r0 + c, :])
        g_last = gcs_col[c - 1:c, :]
        eg = bcast_nat(jnp.exp(gcs_col), GDN_DECAY_LANE)
        ekd = bcast_nat(jnp.exp(g_last - gcs_col), GDN_DECAY_LANE)
        cdec = bcast_nat(jnp.exp(g_last), GDN_DECAY_LANE)
        bfull = bcast_nat(beta[r0:r0 + c, :], GDN_BETA_LANE)
        q = qn_all[r0:r0 + c, :]
        k = kn_all[r0:r0 + c, :]
        kb = k * bfull
        per = LANE // c
        row_tiles = []
        for t in range(pw // LANE):
            tile = gcs_row[GDN_DECAY_LANE + t * per:GDN_DECAY_LANE + t * per + 1, :]
            for j in range(1, per):
                tile = jnp.where(lane_1[0:1] < j * c, tile,
                                 gcs_row[GDN_DECAY_LANE + t * per + j:GDN_DECAY_LANE + t * per + j + 1, :])
            row_tiles.append(tile)
        diff = bcast_packed(gcs_col, GDN_DECAY_LANE) - jnp.concatenate(row_tiles, axis=1)
        decay = jnp.where(incl_p, jnp.exp(jnp.where(incl_p, diff, 0.0)), 0.0)
        kq = _dot_nt(jnp.concatenate([kb, q], axis=0).astype(BF16), block_rows(k.astype(BF16), head_s))
        work.append(dict(
            low=jnp.where(strict_p, kq[:c] * decay, 0.0),
            attn=(kq[c:] * decay).astype(BF16),
            vb=(v_all[r0:r0 + c, :] * bfull).astype(BF16),
            kbe=(kb * eg).astype(BF16),
            qd=(q * eg).astype(BF16),
            kd=(k * ekd).astype(BF16),
            cdec=cdec))

    pmul = lambda a, b: _dot(a, block_rows(b, head_p))
    for wk in work:
        wk["n"] = jnp.where(same_p(GDN_BASE), -wk["low"], 0.0).astype(BF16)
    for wk in work:
        wk["n2"] = pmul(wk["n"], wk["n"]).astype(BF16)
        wk["t"] = eye_p + wk["n"].astype(F32)
    for wk in work:
        wk["t"] = wk["t"] + pmul(wk["t"].astype(BF16), wk["n2"])
        wk["n4"] = pmul(wk["n2"], wk["n2"]).astype(BF16)
    for wk in work:
        wk["t"] = wk["t"] + pmul(wk["t"].astype(BF16), wk["n4"])
    n = GDN_BASE
    while n < c:
        off = same_p(2 * n) & jnp.logical_not(same_p(n))
        for wk in work:
            tb16 = wk["t"].astype(BF16)
            wk["tc"] = (pmul(tb16, jnp.where(off, wk["low"], 0.0).astype(BF16)).astype(BF16), tb16)
        for wk in work:
            tc, tb16 = wk["tc"]
            wk["t"] = wk["t"] - pmul(tc, tb16)
        n *= 2
    for wk in work:
        tb16 = wk["t"].astype(BF16)
        wk["u"] = _dot(tb16, block_rows(wk["vb"], head_s))
        wk["w"] = _dot(tb16, block_rows(wk["kbe"], head_s))

    npair = nh // 2
    pair_w = 2 * LANE
    rr = lax.broadcasted_iota(jnp.int32, (pair_w, pair_w), 0) // LANE
    cc = lax.broadcasted_iota(jnp.int32, (pair_w, pair_w), 1) // LANE
    diag_blocks = rr == cc
    states = [st_ref[p] for p in range(npair)]
    for ci in range(nchunk):
        rows = pl.ds(ci * c, c)
        wk = work[ci]
        ws_parts = []
        for p in range(npair):
            pc = slice(p * pair_w, (p + 1) * pair_w)
            lhs = jnp.concatenate([wk["w"][:, pc].astype(BF16), wk["qd"][:, pc]], axis=0)
            ws_parts.append(_dot(lhs, states[p].astype(BF16)))
        ws = jnp.concatenate(ws_parts, axis=1)
        v_new = (wk["u"] - ws[:c]).astype(BF16)
        o = ws[c:] + _dot(wk["attn"], block_rows(v_new, head_s))
        for p in range(npair):
            pc = slice(p * pair_w, (p + 1) * pair_w)
            upd = _dot_tn(wk["kd"][:, pc], v_new[:, pc])
            states[p] = states[p] * wk["cdec"][:, pc] + jnp.where(diag_blocks, upd, 0.0)
        outs = [_rms_rows(o[:, h * LANE:(h + 1) * LANE]) * nw_ref[...] for h in range(nh)]
        o_ref[rows, :] = (jnp.concatenate(outs, axis=1) * gate_all[ci * c:(ci + 1) * c, :]).astype(BF16)
    for p in range(npair):
        st_ref[p] = states[p]


def _gdn(x, mix_norm_w, w, conv_w, dt_bias, a_log, norm_w, batch, tb=512):
    tb = min(tb, x.shape[0] // batch)
    hp = jnp.zeros((8, LANE), F32).at[0:2, GDN_DECAY_LANE:GDN_DECAY_LANE + GDN_HEADS].set(
        jnp.stack([dt_bias, a_log]).astype(F32))
    cws = [conv_w[:, i * SEG:(i + 1) * SEG] for i in range(3)]
    scratch = [pltpu.VMEM((tb + CONV_PAD, SEG), F32)] * 3 + [pltpu.VMEM((GDN_HEADS // 2, 2 * LANE, 2 * LANE), F32)]
    return _mixer_call(functools.partial(_gdn_kernel, nchunk=tb // GDN_CHUNK), "gdn", x, mix_norm_w, w,
                       [], cws + [hp, norm_w[None, :]], scratch, batch, tb, pipelined=True)


def _mlp_kernel(x_ref, y0_ref, y1_ref, y2_ref, wo_ref, nw_ref, wu_ref, wd_ref, fw_ref, o_ref, xn_ref, acc_ref, *,
                final_norm):
    j = pl.program_id(1)

    @pl.when(j == 0)
    def _():
        x1 = x_ref[...]
        for i, y_ref in enumerate((y0_ref, y1_ref, y2_ref)):
            x1 = x1 + _dot(y_ref[...], wo_ref[pl.ds(i * SEG, SEG), :])
        xn_ref[...] = (_rms_rows(x1) * nw_ref[...]).astype(BF16)
        acc_ref[...] = x1

    h = jnp.maximum(_dot(xn_ref[...], wu_ref[...]), 0.0)
    acc_ref[...] += _dot((h * h).astype(BF16), wd_ref[...])

    @pl.when(j == pl.num_programs(1) - 1)
    def _():
        out = acc_ref[...]
        if final_norm:
            out = _rms_rows(out) * fw_ref[...]
        o_ref[...] = out


def _mlp(x, ys, w_out, norm_w, w_up, w_down, final_w, final_norm, tm=1024, tf=1024):
    t, d = x.shape
    ff = w_up.shape[1]
    tm = min(tm, t)
    rows = lambda width: pl.BlockSpec((tm, width), lambda i, j: (i, 0))
    full = lambda a: pl.BlockSpec(a.shape, lambda i, j: (0,) * a.ndim)
    return pl.pallas_call(
        functools.partial(_mlp_kernel, final_norm=final_norm),
        grid=(t // tm, ff // tf),
        in_specs=[rows(d), rows(SEG), rows(SEG), rows(SEG), full(w_out), full(norm_w),
                  pl.BlockSpec((d, tf), lambda i, j: (0, j)),
                  pl.BlockSpec((tf, d), lambda i, j: (j, 0)),
                  full(final_w)],
        out_specs=rows(d),
        out_shape=jax.ShapeDtypeStruct((t, d), F32),
        scratch_shapes=[pltpu.VMEM((tm, d), BF16), pltpu.VMEM((tm, d), F32)],
        compiler_params=_params("parallel", "arbitrary"),
        name="mlp",
    )(x, *ys, w_out, norm_w, w_up, w_down, final_w)


def _split_w_in(w):
    sizes = (SEG, SEG, SEG, SEG, SEG, 2 * SEG, SSD_HEADS, 3 * SEG, SEG, GDN_HEADS, GDN_HEADS)
    offs = np.concatenate([[0], np.cumsum(sizes)])
    piece = lambda i: w[:, offs[i]:offs[i + 1]]
    small = jnp.concatenate([piece(6), piece(9), piece(10)], axis=1)
    small = jnp.pad(small, ((0, 0), (0, LANE - small.shape[1])))
    slab = lambda idx: jnp.concatenate([piece(i) for i in idx], axis=1)
    w_ret = slab((0, 1, 2, 3))
    w_ssd = jnp.concatenate([slab((4, 5)), small], axis=1)
    w_gdn = jnp.concatenate([slab((7, 8)), small], axis=1)
    return w_ret.astype(BF16), w_ssd.astype(BF16), w_gdn.astype(BF16)


def kernel(x, positions, mix_norm_w, w_in, ret_norm_w, ssd_conv_w, ssd_conv_b, ssd_dt_bias, ssd_a_log, ssd_d,
           ssd_norm_w, gdn_conv_w, gdn_dt_bias, gdn_a_log, gdn_norm_w, w_out, mlp_norm_w, w_up, w_down,
           final_norm_w):
    batch, s, d = x.shape
    depth = w_in.shape[0]
    xf = x.reshape(batch * s, d)
    cosf, sinf = _rope_tables(positions)
    for l in range(depth):
        w_ret, w_ssd, w_gdn = _split_w_in(w_in[l])
        mnw = mix_norm_w[l][None, :]
        y_ret = _retention(xf, mnw, w_ret, cosf, sinf, ret_norm_w[l][None, :], batch)
        y_ssd = _ssd(xf, mnw, w_ssd, ssd_conv_w[l], ssd_conv_b[l], ssd_dt_bias[l], ssd_a_log[l], ssd_d[l],
                     ssd_norm_w[l][None, :], batch)
        y_gdn = _gdn(xf, mnw, w_gdn, gdn_conv_w[l], gdn_dt_bias[l], gdn_a_log[l], gdn_norm_w[l], batch)
        xf = _mlp(xf, (y_ret, y_ssd, y_gdn), w_out[l].astype(BF16), mlp_norm_w[l][None, :], w_up[l].astype(BF16),
                  w_down[l].astype(BF16), final_norm_w[None, :], final_norm=(l == depth - 1))
    return xf.reshape(batch, s, d)
```

```python
import functools

import numpy as np
import jax
import jax.numpy as jnp
from jax import lax
from jax.experimental import pallas as pl
from jax.experimental.pallas import tpu as pltpu

F32 = jnp.float32
BF16 = jnp.bfloat16

NORM_EPS = 1e-6
L2_EPS = 1e-6
ROPE_BASE = 10000.0

RET_HEADS = 4
RET_CHUNK = 128
SSD_HEADS = 8
SSD_GROUPS = 2
SSD_CHUNK = 128
GDN_HEADS = 4
GDN_CHUNK = 64
GDN_BASE = 8
CONV_K = 4

LANE = 128
SEG = 512
CONV_PAD = 8
MLP_SUB = 1024
VMEM_LIMIT = 60000 * 1024

GDN_BETA_LANE = SSD_HEADS
GDN_DECAY_LANE = SSD_HEADS + GDN_HEADS

RET_COL = 0
SSD_COL = RET_COL + 4 * SEG
GDN_COL = SSD_COL + 3 * SEG + LANE
PROJ_COLS = GDN_COL + 4 * SEG + LANE
MIX_COLS = 3 * SEG


def _dot(a, b):
    return jnp.dot(a, b, preferred_element_type=F32)


def _dot_nt(a, b):
    return lax.dot_general(a, b, (((1,), (1,)), ((), ())), preferred_element_type=F32)


def _dot_tn(a, b):
    return lax.dot_general(a, b, (((0,), (0,)), ((), ())), preferred_element_type=F32)


def _split3(x):
    hi = x.astype(BF16)
    r = x - hi.astype(F32)
    mid = r.astype(BF16)
    lo = (r - mid.astype(F32)).astype(BF16)
    return hi, mid, lo


def _cumsum_pair(tri3, tri3_t, x):
    parts = jnp.concatenate(_split3(x), axis=0)
    return _dot(tri3, parts), _dot_tn(parts, tri3_t)


def _expand_heads(x, e3_ref, nheads):
    hi, mid, lo = _split3(x)
    lane = lax.broadcasted_iota(jnp.int32, x.shape, 1)
    packed = jnp.where(lane < nheads, hi.astype(F32),
                       jnp.where(lane < 2 * nheads, pltpu.roll(mid.astype(F32), nheads, 1),
                                 jnp.where(lane < 3 * nheads, pltpu.roll(lo.astype(F32), 2 * nheads, 1), 0.0)))
    return _dot(packed.astype(BF16), e3_ref[...])


def _silu(t):
    return t * jax.nn.sigmoid(t)


def _softplus(t):
    return jnp.maximum(t, 0.0) + jnp.log1p(jnp.exp(-jnp.abs(t)))


def _rms_rows(t, eps=NORM_EPS):
    return t * lax.rsqrt(jnp.mean(t * t, axis=-1, keepdims=True) + eps)


def _tri(n, strict=False, upper=False):
    r = lax.broadcasted_iota(jnp.int32, (n, n), 0)
    c = lax.broadcasted_iota(jnp.int32, (n, n), 1)
    if upper:
        r, c = c, r
    return (r > c) if strict else (r >= c)


def _cols(ref, c0, width=SEG):
    return ref.at[:, pl.ds(c0, width)]


def _rope_kernel(pos_ref, freq_ref, cos_ref, sin_ref):
    ang = pos_ref[...].astype(F32) * freq_ref[...]
    lane = lax.broadcasted_iota(jnp.int32, ang.shape, 1)
    cos_ref[...] = jnp.cos(ang)
    sin_ref[...] = jnp.where(lane < LANE // 2, -jnp.sin(ang), jnp.sin(ang))


def _rope_tables(positions, tb=1024):
    t = positions.size
    half = LANE // 2
    inv_freq = ROPE_BASE ** (-jnp.arange(half, dtype=F32) / half)
    freq = jnp.concatenate([inv_freq, inv_freq])[None, :]
    tb = min(tb, t)
    return pl.pallas_call(
        _rope_kernel,
        grid=(t // tb,),
        in_specs=[pl.BlockSpec((tb, 1), lambda i: (i, 0)), pl.BlockSpec((1, LANE), lambda i: (0, 0))],
        out_specs=[pl.BlockSpec((tb, LANE), lambda i: (i, 0))] * 2,
        out_shape=[jax.ShapeDtypeStruct((t, LANE), F32)] * 2,
        compiler_params=pltpu.CompilerParams(dimension_semantics=("parallel",), vmem_limit_bytes=VMEM_LIMIT),
        name="rope_tables",
    )(positions.reshape(t, 1), freq)


def _slab_bounds():
    bounds = [(RET_COL + j * SEG, SEG) for j in range(4)]
    bounds += [(SSD_COL + j * SEG, SEG) for j in range(3)] + [(SSD_COL + 3 * SEG, LANE)]
    bounds += [(GDN_COL + j * SEG, SEG) for j in range(4)] + [(GDN_COL + 4 * SEG, LANE)]
    return bounds


def _project_slabs(x_ref, nw_ref, w_ref, proj_ref):
    xn = (_rms_rows(x_ref[...]) * nw_ref[...]).astype(BF16)

    def slab(c0, width):
        proj_ref[:, pl.ds(c0, width)] = _dot(xn, w_ref[:, pl.ds(c0, width)])

    return {c0: functools.partial(slab, c0, width) for c0, width in _slab_bounds()}


def _causal_conv(x_ref, ext_ref, w_ref, first):
    tb = x_ref.shape[0]

    @pl.when(first)
    def _():
        ext_ref[pl.ds(0, CONV_PAD), :] = jnp.zeros((CONV_PAD, ext_ref.shape[1]), F32)

    x = x_ref[...]
    ext_ref[pl.ds(CONV_PAD, tb), :] = x
    acc = x * w_ref[CONV_K - 1:CONV_K, :]
    for j in range(CONV_K - 1):
        shift = CONV_K - 1 - j
        acc = acc + ext_ref[pl.ds(CONV_PAD - shift, tb), :] * w_ref[j:j + 1, :]
    ext_ref[pl.ds(0, CONV_PAD), :] = x[tb - CONV_PAD:, :]
    return acc


def _ret_consts():
    c = RET_CHUNK
    h = RET_HEADS
    log_gamma = jnp.log1p(-jnp.exp2(-5.0 - jnp.arange(h, dtype=F32)))
    idx = jnp.arange(c, dtype=F32)
    rel = idx[:, None] - idx[None, :]
    causal = rel >= 0
    d_intra = jnp.where(causal, jnp.exp(log_gamma[:, None, None] * jnp.where(causal, rel, 0.0)), 0.0)
    zeta = jnp.exp(log_gamma[:, None] * (c - 1 - idx))
    xi = jnp.exp(log_gamma[:, None] * (idx + 1))
    cdec = jnp.exp(log_gamma * c)
    ones = jnp.ones((h, c, LANE), F32)
    tab = jnp.stack([zeta[:, :, None] * ones, xi[:, :, None] * ones, cdec[:, None, None] * ones], axis=1)
    return d_intra, tab


def _ret_steps(proj_ref, slabs, cos_ref, sin_ref, dm_ref, tab_ref, nw_ref, o_ref, st_ref, first, nchunk):
    c = RET_CHUNK

    @pl.when(first)
    def _():
        st_ref[...] = jnp.zeros_like(st_ref)

    q_ref, k_ref, v_ref, g_ref = (_cols(proj_ref, RET_COL + j * SEG) for j in range(4))
    gate_all = _silu(g_ref[...])
    slabs[RET_COL + 3 * SEG]()
    yield
    scale = LANE ** -0.5
    work = []
    for ci in range(nchunk):
        rows = pl.ds(ci * c, c)
        cos = cos_ref[rows, :]
        sin = sin_ref[rows, :]
        for h in range(RET_HEADS):
            cols = pl.ds(h * LANE, LANE)
            q = q_ref[rows, cols]
            k = k_ref[rows, cols]
            vb = v_ref[rows, cols].astype(BF16)
            q = (q * cos + pltpu.roll(q, LANE // 2, 1) * sin) * scale
            k = k * cos + pltpu.roll(k, LANE // 2, 1) * sin
            qb = q.astype(BF16)
            scores = _dot_nt(qb, k.astype(BF16)) * dm_ref[h]
            work.append(dict(qb=qb, y=_dot(scores.astype(BF16), vb),
                             kv=_dot_tn((k * tab_ref[h, 0]).astype(BF16), vb)))
        yield
    pending = [slabs[RET_COL + j * SEG] for j in range(3)]
    states = [st_ref[h] for h in range(RET_HEADS)]
    for ci in range(nchunk):
        rows = pl.ds(ci * c, c)
        for h in range(RET_HEADS):
            cols = pl.ds(h * LANE, LANE)
            wk = work[ci * RET_HEADS + h]
            y = wk["y"] + _dot(wk["qb"], states[h].astype(BF16)) * tab_ref[h, 1]
            states[h] = tab_ref[h, 2] * states[h] + wk["kv"]
            gate = gate_all[ci * c:(ci + 1) * c, h * LANE:(h + 1) * LANE]
            o_ref[rows, cols] = (_rms_rows(y) * nw_ref[:, cols] * gate).astype(BF16)
        if pending:
            pending.pop(0)()
        yield
    for slab in pending:
        slab()
    for h in range(RET_HEADS):
        st_ref[h] = states[h]


def _ssd_steps(proj_ref, slabs, cwx_ref, cwbc_ref, cbx_ref, cbbc_ref, hp_ref, dfull_ref, e_ref, nw_ref,
               o_ref, extx_ref, extbc_ref, st_ref, first, nchunk):
    c = SSD_CHUNK
    nh = SSD_HEADS

    @pl.when(first)
    def _():
        st_ref[...] = jnp.zeros_like(st_ref)

    z_ref, x_ref, bc_ref = (_cols(proj_ref, SSD_COL + j * SEG) for j in range(3))
    sm_all = _cols(proj_ref, SSD_COL + 3 * SEG, LANE)[...]
    slabs[SSD_COL + 3 * SEG]()
    xs_all = _silu(_causal_conv(x_ref, extx_ref, cwx_ref, first) + cbx_ref[...])
    slabs[SSD_COL + SEG]()
    yield
    bc_all = _silu(_causal_conv(bc_ref, extbc_ref, cwbc_ref, first) + cbbc_ref[...])
    slabs[SSD_COL + 2 * SEG]()
    yield
    gate_all = _silu(z_ref[...])
    slabs[SSD_COL]()
    yield
    dt_bias = hp_ref[0:1, :]
    a_neg = -jnp.exp(hp_ref[1:2, :])
    d_full = dfull_ref[...]
    causal = _tri(c)
    tri3 = jnp.concatenate([causal.astype(BF16)] * 3, axis=1)
    tri3_t = jnp.concatenate([_tri(c, upper=True).astype(BF16)] * 3, axis=0)
    lane = lax.broadcasted_iota(jnp.int32, (c, LANE), 1)
    gw = SEG // SSD_GROUPS
    ks = LANE
    states = [st_ref[g] for g in range(SSD_GROUPS)]

    for ci in range(nchunk):
        r0 = ci * c
        xs = xs_all[r0:r0 + c, :]
        bc = bc_all[r0:r0 + c, :]
        dt = _softplus(sm_all[r0:r0 + c, :] + dt_bias)
        acs_col, acs_row = _cumsum_pair(tri3, tri3_t, dt * a_neg)
        a_last = acs_col[c - 1:c, :]
        dt_full = _expand_heads(dt, e_ref, nh)
        ea_full = _expand_heads(jnp.exp(acs_col), e_ref, nh)
        te_full = _expand_heads(jnp.exp(a_last - acs_col), e_ref, nh)
        cd_full = ea_full[c - 1:c, :]
        xdt = xs * dt_full
        yield
        y_parts = []
        for g in range(SSD_GROUPS):
            bm = bc[:, g * ks:(g + 1) * ks].astype(BF16)
            cm = bc[:, SSD_GROUPS * ks + g * ks:SSD_GROUPS * ks + (g + 1) * ks].astype(BF16)
            cb = _dot_nt(cm, bm)
            gcols = slice(g * gw, (g + 1) * gw)
            prev = states[g]
            y_inter = _dot(cm, prev.astype(BF16)) * ea_full[:, gcols]
            states[g] = cd_full[:, gcols] * prev + _dot_tn(bm, (xdt[:, gcols] * te_full[:, gcols]).astype(BF16))
            for pair in range(gw // LANE):
                ms = []
                for sub in range(2):
                    hd = g * (nh // SSD_GROUPS) + pair * 2 + sub
                    seg = acs_col[:, hd:hd + 1] - acs_row[hd:hd + 1, :]
                    dec = jnp.where(causal, jnp.exp(jnp.where(causal, seg, 0.0)), 0.0)
                    ms.append((cb * dec).astype(BF16))
                xp = xdt[:, g * gw + pair * LANE:g * gw + (pair + 1) * LANE]
                lo = jnp.where(lane < LANE // 2, xp, 0.0).astype(BF16)
                hi = jnp.where(lane >= LANE // 2, xp, 0.0).astype(BF16)
                y_pair = _dot(jnp.concatenate(ms, axis=1), jnp.concatenate([lo, hi], axis=0))
                y_parts.append(y_pair + y_inter[:, pair * LANE:(pair + 1) * LANE])
            yield
        y = jnp.concatenate(y_parts, axis=1) + xs * d_full
        y = y * gate_all[r0:r0 + c, :]
        outs = [_rms_rows(y[:, g * gw:(g + 1) * gw]) for g in range(SSD_GROUPS)]
        o_ref[pl.ds(r0, c), :] = (jnp.concatenate(outs, axis=1) * nw_ref[...]).astype(BF16)
    for g in range(SSD_GROUPS):
        st_ref[g] = states[g]


def _gdn_steps(proj_ref, slabs, cwq_ref, cwk_ref, cwv_ref, hp_ref, nw_ref,
               o_ref, extq_ref, extk_ref, extv_ref, st_ref, first, nchunk):
    c = GDN_CHUNK
    nh = GDN_HEADS

    @pl.when(first)
    def _():
        st_ref[...] = jnp.zeros_like(st_ref)

    q_ref, k_ref, v_ref, z_ref = (_cols(proj_ref, GDN_COL + j * SEG) for j in range(4))
    sm = _cols(proj_ref, GDN_COL + 4 * SEG, LANE)[...]
    slabs[GDN_COL + 4 * SEG]()
    q_all = _silu(_causal_conv(q_ref, extq_ref, cwq_ref, first))
    slabs[GDN_COL]()
    yield
    k_all = _silu(_causal_conv(k_ref, extk_ref, cwk_ref, first))
    slabs[GDN_COL + SEG]()
    yield
    v_all = _silu(_causal_conv(v_ref, extv_ref, cwv_ref, first))
    slabs[GDN_COL + 2 * SEG]()
    yield
    gate_all = _silu(z_ref[...])
    slabs[GDN_COL + 3 * SEG]()
    dt_bias = hp_ref[0:1, :]
    a_neg = -jnp.exp(hp_ref[1:2, :])
    scale = LANE ** -0.5
    pw = nh * c
    beta = jax.nn.sigmoid(sm)
    g = a_neg * _softplus(sm + dt_bias)
    tri3 = jnp.concatenate([_tri(c).astype(BF16)] * 3, axis=1)
    tri3_t = jnp.concatenate([_tri(c, upper=True).astype(BF16)] * 3, axis=0)
    tri3_t2 = jnp.concatenate([tri3_t] * (LANE // c), axis=1)

    row_p = lax.broadcasted_iota(jnp.int32, (c, pw), 0)
    lane_p = lax.broadcasted_iota(jnp.int32, (c, pw), 1)
    col_p = jnp.bitwise_and(lane_p, c - 1)
    incl_p = row_p >= col_p
    strict_p = row_p > col_p
    eye_p = (row_p == col_p).astype(F32)
    same_p = lambda n: (row_p // n) == (col_p // n)
    lane_1 = lax.broadcasted_iota(jnp.int32, (c, LANE), 1)
    head_p = [((lane_p // c) == h).astype(BF16) for h in range(nh)]
    lane_s = lax.broadcasted_iota(jnp.int32, (c, SEG), 1)
    head_s = [((lane_s // LANE) == h).astype(BF16) for h in range(nh)]

    def block_rows(y, masks):
        return jnp.concatenate([y * m for m in masks], axis=0)

    def bcast_nat(x, lane0):
        r = x.shape[0]
        return jnp.concatenate([jnp.broadcast_to(x[:, lane0 + h:lane0 + h + 1], (r, LANE)) for h in range(nh)], axis=1)

    def bcast_packed(x, lane0):
        tiles = []
        for t in range(pw // LANE):
            per = LANE // c
            tile = jnp.broadcast_to(x[:, lane0 + t * per:lane0 + t * per + 1], (c, LANE))
            for j in range(1, per):
                tile = jnp.where(lane_1 < j * c, tile,
                                 jnp.broadcast_to(x[:, lane0 + t * per + j:lane0 + t * per + j + 1], (c, LANE)))
            tiles.append(tile)
        return jnp.concatenate(tiles, axis=1)

    def l2n(t):
        parts = []
        for h in range(nh):
            th = t[:, h * LANE:(h + 1) * LANE]
            parts.append(th * lax.rsqrt(jnp.sum(th * th, axis=-1, keepdims=True) + L2_EPS))
        return jnp.concatenate(parts, axis=1)

    qn_all = l2n(q_all) * scale
    yield
    kn_all = l2n(k_all)
    yield

    work = []
    for ci in range(nchunk):
        r0 = ci * c
        gcs_col, gcs_row = _cumsum_pair(tri3, tri3_t2, g[r0:r0 + c, :])
        g_last = gcs_col[c - 1:c, :]
        eg = bcast_nat(jnp.exp(gcs_col), GDN_DECAY_LANE)
        ekd = bcast_nat(jnp.exp(g_last - gcs_col), GDN_DECAY_LANE)
        cdec = bcast_nat(jnp.exp(g_last), GDN_DECAY_LANE)
        bfull = bcast_nat(beta[r0:r0 + c, :], GDN_BETA_LANE)
        q = qn_all[r0:r0 + c, :]
        k = kn_all[r0:r0 + c, :]
        kb = k * bfull
        per = LANE // c
        row_tiles = []
        for t in range(pw // LANE):
            tile = gcs_row[GDN_DECAY_LANE + t * per:GDN_DECAY_LANE + t * per + 1, :]
            for j in range(1, per):
                tile = jnp.where(lane_1[0:1] < j * c, tile,
                                 gcs_row[GDN_DECAY_LANE + t * per + j:GDN_DECAY_LANE + t * per + j + 1, :])
            row_tiles.append(tile)
        diff = bcast_packed(gcs_col, GDN_DECAY_LANE) - jnp.concatenate(row_tiles, axis=1)
        decay = jnp.where(incl_p, jnp.exp(jnp.where(incl_p, diff, 0.0)), 0.0)
        kq = _dot_nt(jnp.concatenate([kb, q], axis=0).astype(BF16), block_rows(k.astype(BF16), head_s))
        work.append(dict(
            low=jnp.where(strict_p, kq[:c] * decay, 0.0),
            attn=(kq[c:] * decay).astype(BF16),
            vb=(v_all[r0:r0 + c, :] * bfull).astype(BF16),
            kbe=(kb * eg).astype(BF16),
            qd=(q * eg).astype(BF16),
            kd=(k * ekd).astype(BF16),
            cdec=cdec))
        yield

    pmul = lambda a, b: _dot(a, block_rows(b, head_p))
    for wk in work:
        wk["n"] = jnp.where(same_p(GDN_BASE), -wk["low"], 0.0).astype(BF16)
    for wk in work:
        wk["n2"] = pmul(wk["n"], wk["n"]).astype(BF16)
        wk["t"] = eye_p + wk["n"].astype(F32)
    yield
    for wk in work:
        wk["t"] = wk["t"] + pmul(wk["t"].astype(BF16), wk["n2"])
        wk["n4"] = pmul(wk["n2"], wk["n2"]).astype(BF16)
    yield
    for wk in work:
        wk["t"] = wk["t"] + pmul(wk["t"].astype(BF16), wk["n4"])
    yield
    n = GDN_BASE
    while n < c:
        off = same_p(2 * n) & jnp.logical_not(same_p(n))
        for wk in work:
            tb16 = wk["t"].astype(BF16)
            wk["tc"] = (pmul(tb16, jnp.where(off, wk["low"], 0.0).astype(BF16)).astype(BF16), tb16)
        yield
        for wk in work:
            tc, tb16 = wk["tc"]
            wk["t"] = wk["t"] - pmul(tc, tb16)
        yield
        n *= 2
    for wk in work:
        tb16 = wk["t"].astype(BF16)
        wk["u"] = _dot(tb16, block_rows(wk["vb"], head_s))
        wk["w"] = _dot(tb16, block_rows(wk["kbe"], head_s))
    yield

    npair = nh // 2
    pair_w = 2 * LANE
    rr = lax.broadcasted_iota(jnp.int32, (pair_w, pair_w), 0) // LANE
    cc = lax.broadcasted_iota(jnp.int32, (pair_w, pair_w), 1) // LANE
    diag_blocks = rr == cc
    states = [st_ref[p] for p in range(npair)]
    for ci in range(nchunk):
        rows = pl.ds(ci * c, c)
        wk = work[ci]
        ws_parts = []
        for p in range(npair):
            pc = slice(p * pair_w, (p + 1) * pair_w)
            lhs = jnp.concatenate([wk["w"][:, pc].astype(BF16), wk["qd"][:, pc]], axis=0)
            ws_parts.append(_dot(lhs, states[p].astype(BF16)))
        ws = jnp.concatenate(ws_parts, axis=1)
        v_new = (wk["u"] - ws[:c]).astype(BF16)
        o = ws[c:] + _dot(wk["attn"], block_rows(v_new, head_s))
        for p in range(npair):
            pc = slice(p * pair_w, (p + 1) * pair_w)
            upd = _dot_tn(wk["kd"][:, pc], v_new[:, pc])
            states[p] = states[p] * wk["cdec"][:, pc] + jnp.where(diag_blocks, upd, 0.0)
        outs = [_rms_rows(o[:, h * LANE:(h + 1) * LANE]) * nw_ref[...] for h in range(nh)]
        o_ref[rows, :] = (jnp.concatenate(outs, axis=1) * gate_all[ci * c:(ci + 1) * c, :]).astype(BF16)
        yield
    for p in range(npair):
        st_ref[p] = states[p]


def _interleave(steppers):
    alive = list(steppers)
    while alive:
        still = []
        for gen, per_turn in alive:
            done = False
            for _ in range(per_turn):
                try:
                    next(gen)
                except StopIteration:
                    done = True
                    break
            if not done:
                still.append((gen, per_turn))
        alive = still


def _mix_kernel(x0_ref, xnext_ref, mnw_ref, w_ref, cos_ref, sin_ref,
                r_dm, r_tab, r_nw,
                s_cwx, s_cwbc, s_cbx, s_cbbc, s_hp, s_dfull, s_e, s_nw,
                g_cwq, g_cwk, g_cwv, g_hp, g_nw,
                o_ref, proj_ref, r_st, s_extx, s_extbc, s_st, g_extq, g_extk, g_extv, g_st, *, tb):
    first = pl.program_id(1) == 0

    @pl.when((pl.program_id(0) == 0) & first)
    def _():
        for thunk in _project_slabs(x0_ref, mnw_ref, w_ref, proj_ref).values():
            thunk()

    slabs = _project_slabs(xnext_ref, mnw_ref, w_ref, proj_ref)
    ret = _ret_steps(proj_ref, slabs, cos_ref, sin_ref, r_dm, r_tab, r_nw, _cols(o_ref, 0), r_st, first,
                     tb // RET_CHUNK)
    ssd = _ssd_steps(proj_ref, slabs, s_cwx, s_cwbc, s_cbx, s_cbbc, s_hp, s_dfull, s_e, s_nw,
                     _cols(o_ref, SEG), s_extx, s_extbc, s_st, first, tb // SSD_CHUNK)
    gdn = _gdn_steps(proj_ref, slabs, g_cwq, g_cwk, g_cwv, g_hp, g_nw,
                     _cols(o_ref, 2 * SEG), g_extq, g_extk, g_extv, g_st, first, tb // GDN_CHUNK)
    _interleave([(gdn, 2), (ssd, 1), (ret, 1)])


def _pad_rows(a, rows=8, cols=LANE):
    out = jnp.zeros((rows, cols), F32)
    return out.at[:a.shape[0], :a.shape[1]].set(a.astype(F32))


def _mixers(x, mix_norm_w, w, cosf, sinf, ret_norm_w, ssd_conv_w, ssd_conv_b, ssd_dt_bias, ssd_a_log, ssd_d,
            ssd_norm_w, gdn_conv_w, gdn_dt_bias, gdn_a_log, gdn_norm_w, batch, tb=512):
    t, d = x.shape
    tb = min(tb, t // batch)
    nb = t // batch // tb
    last = t // tb - 1

    d_intra, tab = _ret_consts()
    hw = SEG // SSD_HEADS
    s_hp = _pad_rows(jnp.stack([ssd_dt_bias, ssd_a_log]))
    s_dfull = jnp.repeat(ssd_d.astype(F32), hw)[None, :]
    expand = jnp.tile(jnp.repeat(jnp.eye(SSD_HEADS, dtype=BF16), hw, axis=1), (3, 1))
    expand = jnp.pad(expand, ((0, LANE - 3 * SSD_HEADS), (0, 0)))
    g_hp = jnp.zeros((8, LANE), F32).at[0:2, GDN_DECAY_LANE:GDN_DECAY_LANE + GDN_HEADS].set(
        jnp.stack([gdn_dt_bias, gdn_a_log]).astype(F32))
    consts = [d_intra, tab, ret_norm_w[None, :],
              ssd_conv_w[:, :SEG], ssd_conv_w[:, SEG:], ssd_conv_b[None, :SEG], ssd_conv_b[None, SEG:],
              s_hp, s_dfull, expand, ssd_norm_w[None, :]]
    consts += [gdn_conv_w[:, i * SEG:(i + 1) * SEG] for i in range(3)] + [g_hp, gdn_norm_w[None, :]]

    rows = lambda width: pl.BlockSpec((tb, width), lambda b, i: (b * nb + i, 0))
    full = lambda a: pl.BlockSpec(a.shape, lambda b, i: (0,) * a.ndim)
    once = lambda a: pl.BlockSpec(a.shape, lambda b, i: (0,) * a.ndim, pipeline_mode=pl.Buffered(1))
    in_specs = [pl.BlockSpec((tb, d), lambda b, i: (0, 0), pipeline_mode=pl.Buffered(1)),
                pl.BlockSpec((tb, d), lambda b, i: (jnp.minimum(b * nb + i + 1, last), 0)),
                full(mix_norm_w), once(w), rows(LANE), rows(LANE)] + [full(a) for a in consts]
    conv_hist = pltpu.VMEM((tb + CONV_PAD, SEG), F32)
    scratch = [pltpu.VMEM((tb, PROJ_COLS), F32),
               pltpu.VMEM((RET_HEADS, LANE, LANE), F32),
               conv_hist, conv_hist, pltpu.VMEM((SSD_GROUPS, LANE, SEG // SSD_GROUPS), F32),
               conv_hist, conv_hist, conv_hist, pltpu.VMEM((GDN_HEADS // 2, 2 * LANE, 2 * LANE), F32)]
    return pl.pallas_call(
        functools.partial(_mix_kernel, tb=tb),
        grid=(batch, nb),
        in_specs=in_specs,
        out_specs=rows(MIX_COLS),
        out_shape=jax.ShapeDtypeStruct((t, MIX_COLS), BF16),
        scratch_shapes=scratch,
        compiler_params=pltpu.CompilerParams(dimension_semantics=("arbitrary", "arbitrary"),
                                             vmem_limit_bytes=VMEM_LIMIT),
        name="mixers",
    )(x, x, mix_norm_w, w, cosf, sinf, *consts)


def _mlp_kernel(x_ref, y_ref, wo_ref, nw_ref, wu_ref, wd_ref, fw_ref, o_ref, xn_ref, acc_ref, *, final_norm):
    j = pl.program_id(1)

    @pl.when(j == 0)
    def _():
        x1 = x_ref[...] + _dot(y_ref[...], wo_ref[...])
        xn_ref[...] = (_rms_rows(x1) * nw_ref[...]).astype(BF16)
        acc_ref[...] = x1

    xn = xn_ref[...]
    acc = acc_ref[...]
    for c0 in range(0, wu_ref.shape[1], MLP_SUB):
        h = jnp.maximum(_dot(xn, wu_ref[:, pl.ds(c0, MLP_SUB)]), 0.0)
        acc = acc + _dot((h * h).astype(BF16), wd_ref[pl.ds(c0, MLP_SUB), :])
    acc_ref[...] = acc

    @pl.when(j == pl.num_programs(1) - 1)
    def _():
        out = acc_ref[...]
        if final_norm:
            out = _rms_rows(out) * fw_ref[...]
        o_ref[...] = out


def _mlp(x, y, w_out, norm_w, w_up, w_down, final_w, final_norm, tm=1024, tf=2048):
    t, d = x.shape
    ff = w_up.shape[1]
    tm = min(tm, t)
    rows = lambda width: pl.BlockSpec((tm, width), lambda i, j: (i, 0))
    full = lambda a: pl.BlockSpec(a.shape, lambda i, j: (0,) * a.ndim)
    return pl.pallas_call(
        functools.partial(_mlp_kernel, final_norm=final_norm),
        grid=(t // tm, ff // tf),
        in_specs=[rows(d), rows(y.shape[1]), full(w_out), full(norm_w),
                  pl.BlockSpec((d, tf), lambda i, j: (0, j)),
                  pl.BlockSpec((tf, d), lambda i, j: (j, 0)),
                  full(final_w)],
        out_specs=rows(d),
        out_shape=jax.ShapeDtypeStruct((t, d), F32),
        scratch_shapes=[pltpu.VMEM((tm, d), BF16), pltpu.VMEM((tm, d), F32)],
        compiler_params=pltpu.CompilerParams(dimension_semantics=("parallel", "arbitrary"),
                                             vmem_limit_bytes=VMEM_LIMIT),
        name="mlp",
    )(x, y, w_out, norm_w, w_up, w_down, final_w)


def _arrange_w_in(w):
    sizes = (SEG, SEG, SEG, SEG, SEG, 2 * SEG, SSD_HEADS, 3 * SEG, SEG, GDN_HEADS, GDN_HEADS)
    offs = np.concatenate([[0], np.cumsum(sizes)])
    piece = lambda i: w[:, offs[i]:offs[i + 1]]
    small = jnp.concatenate([piece(6), piece(9), piece(10)], axis=1)
    small = jnp.pad(small, ((0, 0), (0, LANE - small.shape[1])))
    cols = [piece(i) for i in (0, 1, 2, 3, 4, 5)] + [small, piece(7), piece(8), small]
    return jnp.concatenate(cols, axis=1).astype(BF16)


def kernel(x, positions, mix_norm_w, w_in, ret_norm_w, ssd_conv_w, ssd_conv_b, ssd_dt_bias, ssd_a_log, ssd_d,
           ssd_norm_w, gdn_conv_w, gdn_dt_bias, gdn_a_log, gdn_norm_w, w_out, mlp_norm_w, w_up, w_down,
           final_norm_w):
    batch, s, d = x.shape
    depth = w_in.shape[0]
    xf = x.reshape(batch * s, d)
    cosf, sinf = _rope_tables(positions)
    for l in range(depth):
        y = _mixers(xf, mix_norm_w[l][None, :], _arrange_w_in(w_in[l]), cosf, sinf, ret_norm_w[l],
                    ssd_conv_w[l], ssd_conv_b[l], ssd_dt_bias[l], ssd_a_log[l], ssd_d[l], ssd_norm_w[l],
                    gdn_conv_w[l], gdn_dt_bias[l], gdn_a_log[l], gdn_norm_w[l], batch)
        xf = _mlp(xf, y, w_out[l].astype(BF16), mlp_norm_w[l][None, :], w_up[l].astype(BF16),
                  w_down[l].astype(BF16), final_norm_w[None, :], final_norm=(l == depth - 1))
    return xf.reshape(batch, s, d)
```

```python
import functools

import numpy as np
import jax
import jax.numpy as jnp
from jax import lax
from jax.experimental import pallas as pl
from jax.experimental.pallas import tpu as pltpu

F32 = jnp.float32
BF16 = jnp.bfloat16

NORM_EPS = 1e-6
L2_EPS = 1e-6
ROPE_BASE = 10000.0

RET_HEADS = 4
RET_CHUNK = 128
SSD_HEADS = 8
SSD_GROUPS = 2
SSD_CHUNK = 128
GDN_HEADS = 4
GDN_CHUNK = 64
GDN_BASE = 8
CONV_K = 4

LANE = 128
SEG = 512
CONV_PAD = 8
MLP_SUB = 1024
VMEM_LIMIT = 60000 * 1024

GDN_BETA_LANE = SSD_HEADS
GDN_DECAY_LANE = SSD_HEADS + GDN_HEADS


def _dot(a, b):
    return jnp.dot(a, b, preferred_element_type=F32)


def _dot_nt(a, b):
    return lax.dot_general(a, b, (((1,), (1,)), ((), ())), preferred_element_type=F32)


def _dot_tn(a, b):
    return lax.dot_general(a, b, (((0,), (0,)), ((), ())), preferred_element_type=F32)


def _split3(x):
    hi = x.astype(BF16)
    r = x - hi.astype(F32)
    mid = r.astype(BF16)
    lo = (r - mid.astype(F32)).astype(BF16)
    return hi, mid, lo


def _cumsum_pair(tri3, tri3_t, x):
    parts = jnp.concatenate(_split3(x), axis=0)
    return _dot(tri3, parts), _dot_tn(parts, tri3_t)


def _expand_heads(x, e3_ref, nheads):
    hi, mid, lo = _split3(x)
    lane = lax.broadcasted_iota(jnp.int32, x.shape, 1)
    packed = jnp.where(lane < nheads, hi.astype(F32),
                       jnp.where(lane < 2 * nheads, pltpu.roll(mid.astype(F32), nheads, 1),
                                 jnp.where(lane < 3 * nheads, pltpu.roll(lo.astype(F32), 2 * nheads, 1), 0.0)))
    return _dot(packed.astype(BF16), e3_ref[...])


def _silu(t):
    return t * jax.nn.sigmoid(t)


def _softplus(t):
    return jnp.maximum(t, 0.0) + jnp.log1p(jnp.exp(-jnp.abs(t)))


def _rms_rows(t, eps=NORM_EPS):
    return t * lax.rsqrt(jnp.mean(t * t, axis=-1, keepdims=True) + eps)


def _params(*sem):
    return pltpu.CompilerParams(dimension_semantics=sem, vmem_limit_bytes=VMEM_LIMIT)


def _rope_kernel(pos_ref, freq_ref, cos_ref, sin_ref):
    ang = pos_ref[...].astype(F32) * freq_ref[...]
    lane = lax.broadcasted_iota(jnp.int32, ang.shape, 1)
    cos_ref[...] = jnp.cos(ang)
    sin_ref[...] = jnp.where(lane < LANE // 2, -jnp.sin(ang), jnp.sin(ang))


def _rope_tables(positions, tb=1024):
    t = positions.size
    half = LANE // 2
    inv_freq = ROPE_BASE ** (-jnp.arange(half, dtype=F32) / half)
    freq = jnp.concatenate([inv_freq, inv_freq])[None, :]
    tb = min(tb, t)
    return pl.pallas_call(
        _rope_kernel,
        grid=(t // tb,),
        in_specs=[pl.BlockSpec((tb, 1), lambda i: (i, 0)), pl.BlockSpec((1, LANE), lambda i: (0, 0))],
        out_specs=[pl.BlockSpec((tb, LANE), lambda i: (i, 0))] * 2,
        out_shape=[jax.ShapeDtypeStruct((t, LANE), F32)] * 2,
        compiler_params=_params("parallel"),
        name="rope_tables",
    )(positions.reshape(t, 1), freq)


def _project(x_ref, nw_ref, w_ref, proj_ref):
    xn = (_rms_rows(x_ref[...]) * nw_ref[...]).astype(BF16)
    n = w_ref.shape[1]
    for c0 in range(0, n, SEG):
        width = min(SEG, n - c0)
        proj_ref[:, pl.ds(c0, width)] = _dot(xn, w_ref[:, pl.ds(c0, width)])


def _project_slabs(x_ref, nw_ref, w_ref, proj_ref):
    xn = (_rms_rows(x_ref[...]) * nw_ref[...]).astype(BF16)
    n = w_ref.shape[1]

    def slab(c0, width):
        proj_ref[:, pl.ds(c0, width)] = _dot(xn, w_ref[:, pl.ds(c0, width)])

    return [functools.partial(slab, c0, min(SEG, n - c0)) for c0 in range(0, n, SEG)]


def _project_first(x0_ref, nw_ref, w_ref, proj_ref):
    @pl.when((pl.program_id(0) == 0) & (pl.program_id(1) == 0))
    def _():
        _project(x0_ref, nw_ref, w_ref, proj_ref)


def _seg_view(proj_ref, j, width=SEG):
    return proj_ref.at[:, pl.ds(j * SEG, width)]


def _ret_consts():
    c = RET_CHUNK
    h = RET_HEADS
    log_gamma = jnp.log1p(-jnp.exp2(-5.0 - jnp.arange(h, dtype=F32)))
    idx = jnp.arange(c, dtype=F32)
    rel = idx[:, None] - idx[None, :]
    causal = rel >= 0
    d_intra = jnp.where(causal, jnp.exp(log_gamma[:, None, None] * jnp.where(causal, rel, 0.0)), 0.0)
    zeta = jnp.exp(log_gamma[:, None] * (c - 1 - idx))
    xi = jnp.exp(log_gamma[:, None] * (idx + 1))
    cdec = jnp.exp(log_gamma * c)
    ones = jnp.ones((h, c, LANE), F32)
    tab = jnp.stack([zeta[:, :, None] * ones, xi[:, :, None] * ones, cdec[:, None, None] * ones], axis=1)
    return d_intra, tab


def _ret_kernel(x_ref, mnw_ref, w_ref, cos_ref, sin_ref, dm_ref, tab_ref, nw_ref, o_ref, proj_ref, st_ref, *,
                nchunk):
    c = RET_CHUNK

    @pl.when(pl.program_id(1) == 0)
    def _():
        st_ref[...] = jnp.zeros_like(st_ref)

    _project(x_ref, mnw_ref, w_ref, proj_ref)
    q_ref, k_ref, v_ref, g_ref = (_seg_view(proj_ref, j) for j in range(4))

    scale = LANE ** -0.5
    work = []
    for ci in range(nchunk):
        rows = pl.ds(ci * c, c)
        cos = cos_ref[rows, :]
        sin = sin_ref[rows, :]
        for h in range(RET_HEADS):
            cols = pl.ds(h * LANE, LANE)
            q = q_ref[rows, cols]
            k = k_ref[rows, cols]
            vb = v_ref[rows, cols].astype(BF16)
            q = (q * cos + pltpu.roll(q, LANE // 2, 1) * sin) * scale
            k = k * cos + pltpu.roll(k, LANE // 2, 1) * sin
            qb = q.astype(BF16)
            scores = _dot_nt(qb, k.astype(BF16)) * dm_ref[h]
            work.append(dict(qb=qb, y=_dot(scores.astype(BF16), vb),
                             kv=_dot_tn((k * tab_ref[h, 0]).astype(BF16), vb)))
    states = [st_ref[h] for h in range(RET_HEADS)]
    for ci in range(nchunk):
        rows = pl.ds(ci * c, c)
        for h in range(RET_HEADS):
            cols = pl.ds(h * LANE, LANE)
            wk = work[ci * RET_HEADS + h]
            y = wk["y"] + _dot(wk["qb"], states[h].astype(BF16)) * tab_ref[h, 1]
            states[h] = tab_ref[h, 2] * states[h] + wk["kv"]
            o_ref[rows, cols] = (_rms_rows(y) * nw_ref[:, cols] * _silu(g_ref[rows, cols])).astype(BF16)
    for h in range(RET_HEADS):
        st_ref[h] = states[h]


def _mixer_call(body, name, x, mix_norm_w, w, row_inputs, consts, scratch, batch, tb, pipelined):
    t, d = x.shape
    nb = t // batch // tb
    last = t // tb - 1
    rows = lambda width: pl.BlockSpec((tb, width), lambda b, i: (b * nb + i, 0))
    full = lambda a: pl.BlockSpec(a.shape, lambda b, i: (0,) * a.ndim)
    if pipelined:
        x_specs = [pl.BlockSpec((tb, d), lambda b, i: (0, 0)),
                   pl.BlockSpec((tb, d), lambda b, i: (jnp.minimum(b * nb + i + 1, last), 0))]
    else:
        x_specs = [rows(d)]
    in_specs = x_specs + [full(mix_norm_w), full(w)]
    in_specs += [rows(a.shape[1]) for a in row_inputs] + [full(a) for a in consts]
    consts = list(row_inputs) + list(consts)
    return pl.pallas_call(
        body,
        grid=(batch, nb),
        in_specs=in_specs,
        out_specs=rows(SEG),
        out_shape=jax.ShapeDtypeStruct((t, SEG), BF16),
        scratch_shapes=[pltpu.VMEM((tb, w.shape[1]), F32)] + scratch,
        compiler_params=_params("arbitrary", "arbitrary"),
        name=name,
    )(*([x] * len(x_specs)), mix_norm_w, w, *consts)


def _retention(x, mix_norm_w, w, cosf, sinf, norm_w, batch, tb=512):
    tb = min(tb, x.shape[0] // batch)
    d_intra, tab = _ret_consts()
    return _mixer_call(functools.partial(_ret_kernel, nchunk=tb // RET_CHUNK), "retention", x, mix_norm_w, w,
                       [cosf, sinf], [d_intra, tab, norm_w], [pltpu.VMEM((RET_HEADS, LANE, LANE), F32)], batch, tb,
                       pipelined=False)


def _causal_conv(x_ref, ext_ref, w_ref, first):
    tb = x_ref.shape[0]

    @pl.when(first)
    def _():
        ext_ref[pl.ds(0, CONV_PAD), :] = jnp.zeros((CONV_PAD, ext_ref.shape[1]), F32)

    x = x_ref[...]
    ext_ref[pl.ds(CONV_PAD, tb), :] = x
    acc = x * w_ref[CONV_K - 1:CONV_K, :]
    for j in range(CONV_K - 1):
        shift = CONV_K - 1 - j
        acc = acc + ext_ref[pl.ds(CONV_PAD - shift, tb), :] * w_ref[j:j + 1, :]
    ext_ref[pl.ds(0, CONV_PAD), :] = x[tb - CONV_PAD:, :]
    return acc


def _tri(n, strict=False, upper=False):
    r = lax.broadcasted_iota(jnp.int32, (n, n), 0)
    c = lax.broadcasted_iota(jnp.int32, (n, n), 1)
    if upper:
        r, c = c, r
    return (r > c) if strict else (r >= c)


def _ssd_kernel(x0_ref, xnext_ref, mnw_ref, w_ref, cwx_ref, cwbc_ref, cbx_ref, cbbc_ref, hp_ref, dfull_ref, e_ref,
                nw_ref, o_ref, proj_ref, extx_ref, extbc_ref, st_ref, *, nchunk):
    c = SSD_CHUNK
    nh = SSD_HEADS
    first = pl.program_id(1) == 0

    @pl.when(first)
    def _():
        st_ref[...] = jnp.zeros_like(st_ref)

    _project_first(x0_ref, mnw_ref, w_ref, proj_ref)
    z_ref, x_ref, bc_ref = (_seg_view(proj_ref, j) for j in range(3))
    sm_ref = _seg_view(proj_ref, 3, LANE)

    slabs = _project_slabs(xnext_ref, mnw_ref, w_ref, proj_ref)
    sm_all = sm_ref[...]
    xs_all = _silu(_causal_conv(x_ref, extx_ref, cwx_ref, first) + cbx_ref[...])
    slabs[1]()
    bc_all = _silu(_causal_conv(bc_ref, extbc_ref, cwbc_ref, first) + cbbc_ref[...])
    slabs[2]()
    gate_all = _silu(z_ref[...])
    slabs[0]()
    slabs[3]()
    dt_bias = hp_ref[0:1, :]
    a_neg = -jnp.exp(hp_ref[1:2, :])
    d_full = dfull_ref[...]
    causal = _tri(c)
    tri3 = jnp.concatenate([causal.astype(BF16)] * 3, axis=1)
    tri3_t = jnp.concatenate([_tri(c, upper=True).astype(BF16)] * 3, axis=0)
    lane = lax.broadcasted_iota(jnp.int32, (c, LANE), 1)
    gw = SEG // SSD_GROUPS
    ks = LANE
    states = [st_ref[g] for g in range(SSD_GROUPS)]

    for ci in range(nchunk):
        r0 = ci * c
        xs = xs_all[r0:r0 + c, :]
        bc = bc_all[r0:r0 + c, :]
        dt = _softplus(sm_all[r0:r0 + c, :] + dt_bias)
        acs_col, acs_row = _cumsum_pair(tri3, tri3_t, dt * a_neg)
        a_last = acs_col[c - 1:c, :]
        dt_full = _expand_heads(dt, e_ref, nh)
        ea_full = _expand_heads(jnp.exp(acs_col), e_ref, nh)
        te_full = _expand_heads(jnp.exp(a_last - acs_col), e_ref, nh)
        cd_full = ea_full[c - 1:c, :]
        xdt = xs * dt_full
        y_parts = []
        for g in range(SSD_GROUPS):
            bm = bc[:, g * ks:(g + 1) * ks].astype(BF16)
            cm = bc[:, SSD_GROUPS * ks + g * ks:SSD_GROUPS * ks + (g + 1) * ks].astype(BF16)
            cb = _dot_nt(cm, bm)
            gcols = slice(g * gw, (g + 1) * gw)
            prev = states[g]
            y_inter = _dot(cm, prev.astype(BF16)) * ea_full[:, gcols]
            states[g] = cd_full[:, gcols] * prev + _dot_tn(bm, (xdt[:, gcols] * te_full[:, gcols]).astype(BF16))
            for pair in range(gw // LANE):
                ms = []
                for sub in range(2):
                    hd = g * (nh // SSD_GROUPS) + pair * 2 + sub
                    seg = acs_col[:, hd:hd + 1] - acs_row[hd:hd + 1, :]
                    dec = jnp.where(causal, jnp.exp(jnp.where(causal, seg, 0.0)), 0.0)
                    ms.append((cb * dec).astype(BF16))
                xp = xdt[:, g * gw + pair * LANE:g * gw + (pair + 1) * LANE]
                lo = jnp.where(lane < LANE // 2, xp, 0.0).astype(BF16)
                hi = jnp.where(lane >= LANE // 2, xp, 0.0).astype(BF16)
                y_pair = _dot(jnp.concatenate(ms, axis=1), jnp.concatenate([lo, hi], axis=0))
                y_parts.append(y_pair + y_inter[:, pair * LANE:(pair + 1) * LANE])
        y = jnp.concatenate(y_parts, axis=1) + xs * d_full
        y = y * gate_all[r0:r0 + c, :]
        outs = [_rms_rows(y[:, g * gw:(g + 1) * gw]) for g in range(SSD_GROUPS)]
        o_ref[pl.ds(r0, c), :] = (jnp.concatenate(outs, axis=1) * nw_ref[...]).astype(BF16)
    for g in range(SSD_GROUPS):
        st_ref[g] = states[g]


def _pad_rows(a, rows=8, cols=LANE):
    out = jnp.zeros((rows, cols), F32)
    return out.at[:a.shape[0], :a.shape[1]].set(a.astype(F32))


def _ssd(x, mix_norm_w, w, conv_w, conv_b, dt_bias, a_log, d_skip, norm_w, batch, tb=512):
    tb = min(tb, x.shape[0] // batch)
    hp = _pad_rows(jnp.stack([dt_bias, a_log]))
    hw = SEG // SSD_HEADS
    d_full = jnp.repeat(d_skip.astype(F32), hw)[None, :]
    expand = jnp.tile(jnp.repeat(jnp.eye(SSD_HEADS, dtype=BF16), hw, axis=1), (3, 1))
    expand = jnp.pad(expand, ((0, LANE - 3 * SSD_HEADS), (0, 0)))
    cwx, cwbc = conv_w[:, :SEG], conv_w[:, SEG:]
    cbx, cbbc = conv_b[None, :SEG], conv_b[None, SEG:]
    consts = [cwx, cwbc, cbx, cbbc, hp, d_full, expand, norm_w]
    scratch = [pltpu.VMEM((tb + CONV_PAD, SEG), F32), pltpu.VMEM((tb + CONV_PAD, SEG), F32),
               pltpu.VMEM((SSD_GROUPS, LANE, SEG // SSD_GROUPS), F32)]
    return _mixer_call(functools.partial(_ssd_kernel, nchunk=tb // SSD_CHUNK), "ssd", x, mix_norm_w, w,
                       [], consts, scratch, batch, tb, pipelined=True)


def _gdn_kernel(x0_ref, xnext_ref, mnw_ref, w_ref, cwq_ref, cwk_ref, cwv_ref, hp_ref, nw_ref,
                o_ref, proj_ref, extq_ref, extk_ref, extv_ref, st_ref, *, nchunk):
    c = GDN_CHUNK
    nh = GDN_HEADS
    first = pl.program_id(1) == 0

    @pl.when(first)
    def _():
        st_ref[...] = jnp.zeros_like(st_ref)

    _project_first(x0_ref, mnw_ref, w_ref, proj_ref)
    q_ref, k_ref, v_ref, z_ref = (_seg_view(proj_ref, j) for j in range(4))
    sm_ref = _seg_view(proj_ref, 4, LANE)

    slabs = _project_slabs(xnext_ref, mnw_ref, w_ref, proj_ref)
    sm = sm_ref[...]
    q_all = _silu(_causal_conv(q_ref, extq_ref, cwq_ref, first))
    slabs[0]()
    k_all = _silu(_causal_conv(k_ref, extk_ref, cwk_ref, first))
    slabs[1]()
    v_all = _silu(_causal_conv(v_ref, extv_ref, cwv_ref, first))
    slabs[2]()
    gate_all = _silu(z_ref[...])
    slabs[3]()
    slabs[4]()
    dt_bias = hp_ref[0:1, :]
    a_neg = -jnp.exp(hp_ref[1:2, :])
    scale = LANE ** -0.5
    pw = nh * c
    beta = jax.nn.sigmoid(sm)
    g = a_neg * _softplus(sm + dt_bias)
    tri3 = jnp.concatenate([_tri(c).astype(BF16)] * 3, axis=1)
    tri3_t = jnp.concatenate([_tri(c, upper=True).astype(BF16)] * 3, axis=0)
    tri3_t2 = jnp.concatenate([tri3_t] * (LANE // c), axis=1)

    row_p = lax.broadcasted_iota(jnp.int32, (c, pw), 0)
    lane_p = lax.broadcasted_iota(jnp.int32, (c, pw), 1)
    col_p = jnp.bitwise_and(lane_p, c - 1)
    incl_p = row_p >= col_p
    strict_p = row_p > col_p
    eye_p = (row_p == col_p).astype(F32)
    same_p = lambda n: (row_p // n) == (col_p // n)
    lane_1 = lax.broadcasted_iota(jnp.int32, (c, LANE), 1)
    head_p = [((lane_p // c) == h).astype(BF16) for h in range(nh)]
    lane_s = lax.broadcasted_iota(jnp.int32, (c, SEG), 1)
    head_s = [((lane_s // LANE) == h).astype(BF16) for h in range(nh)]

    def block_rows(y, masks):
        return jnp.concatenate([y * m for m in masks], axis=0)

    def bcast_nat(x, lane0):
        r = x.shape[0]
        return jnp.concatenate([jnp.broadcast_to(x[:, lane0 + h:lane0 + h + 1], (r, LANE)) for h in range(nh)], axis=1)

    def bcast_packed(x, lane0):
        tiles = []
        for t in range(pw // LANE):
            per = LANE // c
            tile = jnp.broadcast_to(x[:, lane0 + t * per:lane0 + t * per + 1], (c, LANE))
            for j in range(1, per):
                tile = jnp.where(lane_1 < j * c, tile,
                                 jnp.broadcast_to(x[:, lane0 + t * per + j:lane0 + t * per + j + 1], (c, LANE)))
            tiles.append(tile)
        return jnp.concatenate(tiles, axis=1)

    def l2n(t):
        parts = []
        for h in range(nh):
            th = t[:, h * LANE:(h + 1) * LANE]
            parts.append(th * lax.rsqrt(jnp.sum(th * th, axis=-1, keepdims=True) + L2_EPS))
        return jnp.concatenate(parts, axis=1)

    qn_all = l2n(q_all) * scale
    kn_all = l2n(k_all)

    work = []
    for ci in range(nchunk):
        r0 = ci * c
        gcs_col, gcs_row = _cumsum_pair(tri3, tri3_t2, g[r0:r0 + c, :])
        g_last = gcs_col[c - 1:c, :]
        eg = bcast_nat(jnp.exp(gcs_col), GDN_DECAY_LANE)
        ekd = bcast_nat(jnp.exp(g_last - gcs_col), GDN_DECAY_LANE)
        cdec = bcast_nat(jnp.exp(g_last), GDN_DECAY_LANE)
        bfull = bcast_nat(beta[r0:r0 + c, :], GDN_BETA_LANE)
        q = qn_all[r0:r0 + c, :]
        k = kn_all[r0:r0 + c, :]
        kb = k * bfull
        per = LANE // c
        row_tiles = []
        for t in range(pw // LANE):
            tile = gcs_row[GDN_DECAY_LANE + t * per:GDN_DECAY_LANE + t * per + 1, :]
            for j in range(1, per):
                tile = jnp.where(lane_1[0:1] < j * c, tile,
                                 gcs_row[GDN_DECAY_LANE + t * per + j:GDN_DECAY_LANE + t * per + j + 1, :])
            row_tiles.append(tile)
        diff = bcast_packed(gcs_col, GDN_DECAY_LANE) - jnp.concatenate(row_tiles, axis=1)
        decay = jnp.where(incl_p, jnp.exp(jnp.where(incl_p, diff, 0.0)), 0.0)
        kq = _dot_nt(jnp.concatenate([kb, q], axis=0).astype(BF16), block_rows(k.astype(BF16), head_s))
        work.append(dict(
            low=jnp.where(strict_p, kq[:c] * decay, 0.0),
            attn=(kq[c:] * decay).astype(BF16),
            vb=(v_all[r0:r0 + c, :] * bfull).astype(BF16),
            kbe=(kb * eg).astype(BF16),
            qd=(q * eg).astype(BF16),
            kd=(k * ekd).astype(BF16),
            cdec=cdec))

    pmul = lambda a, b: _dot(a, block_rows(b, head_p))
    for wk in work:
        wk["n"] = jnp.where(same_p(GDN_BASE), -wk["low"], 0.0).astype(BF16)
    for wk in work:
        wk["n2"] = pmul(wk["n"], wk["n"]).astype(BF16)
        wk["t"] = eye_p + wk["n"].astype(F32)
    for wk in work:
        wk["t"] = wk["t"] + pmul(wk["t"].astype(BF16), wk["n2"])
        wk["n4"] = pmul(wk["n2"], wk["n2"]).astype(BF16)
    for wk in work:
        wk["t"] = wk["t"] + pmul(wk["t"].astype(BF16), wk["n4"])
    n = GDN_BASE
    while n < c:
        off = same_p(2 * n) & jnp.logical_not(same_p(n))
        for wk in work:
            tb16 = wk["t"].astype(BF16)
            wk["tc"] = (pmul(tb16, jnp.where(off, wk["low"], 0.0).astype(BF16)).astype(BF16), tb16)
        for wk in work:
            tc, tb16 = wk["tc"]
            wk["t"] = wk["t"] - pmul(tc, tb16)
        n *= 2
    for wk in work:
        tb16 = wk["t"].astype(BF16)
        wk["u"] = _dot(tb16, block_rows(wk["vb"], head_s))
        wk["w"] = _dot(tb16, block_rows(wk["kbe"], head_s))

    npair = nh // 2
    pair_w = 2 * LANE
    rr = lax.broadcasted_iota(jnp.int32, (pair_w, pair_w), 0) // LANE
    cc = lax.broadcasted_iota(jnp.int32, (pair_w, pair_w), 1) // LANE
    diag_blocks = rr == cc
    states = [st_ref[p] for p in range(npair)]
    for ci in range(nchunk):
        rows = pl.ds(ci * c, c)
        wk = work[ci]
        ws_parts = []
        for p in range(npair):
            pc = slice(p * pair_w, (p + 1) * pair_w)
            lhs = jnp.concatenate([wk["w"][:, pc].astype(BF16), wk["qd"][:, pc]], axis=0)
            ws_parts.append(_dot(lhs, states[p].astype(BF16)))
        ws = jnp.concatenate(ws_parts, axis=1)
        v_new = (wk["u"] - ws[:c]).astype(BF16)
        o = ws[c:] + _dot(wk["attn"], block_rows(v_new, head_s))
        for p in range(npair):
            pc = slice(p * pair_w, (p + 1) * pair_w)
            upd = _dot_tn(wk["kd"][:, pc], v_new[:, pc])
            states[p] = states[p] * wk["cdec"][:, pc] + jnp.where(diag_blocks, upd, 0.0)
        outs = [_rms_rows(o[:, h * LANE:(h + 1) * LANE]) * nw_ref[...] for h in range(nh)]
        o_ref[rows, :] = (jnp.concatenate(outs, axis=1) * gate_all[ci * c:(ci + 1) * c, :]).astype(BF16)
    for p in range(npair):
        st_ref[p] = states[p]


def _gdn(x, mix_norm_w, w, conv_w, dt_bias, a_log, norm_w, batch, tb=512):
    tb = min(tb, x.shape[0] // batch)
    hp = jnp.zeros((8, LANE), F32).at[0:2, GDN_DECAY_LANE:GDN_DECAY_LANE + GDN_HEADS].set(
        jnp.stack([dt_bias, a_log]).astype(F32))
    cws = [conv_w[:, i * SEG:(i + 1) * SEG] for i in range(3)]
    scratch = [pltpu.VMEM((tb + CONV_PAD, SEG), F32)] * 3 + [pltpu.VMEM((GDN_HEADS // 2, 2 * LANE, 2 * LANE), F32)]
    return _mixer_call(functools.partial(_gdn_kernel, nchunk=tb // GDN_CHUNK), "gdn", x, mix_norm_w, w,
                       [], cws + [hp, norm_w[None, :]], scratch, batch, tb, pipelined=True)


def _mlp_kernel(x_ref, y0_ref, y1_ref, y2_ref, wo_ref, nw_ref, wu_ref, wd_ref, fw_ref, o_ref, xn_ref, acc_ref, *,
                final_norm):
    j = pl.program_id(1)

    @pl.when(j == 0)
    def _():
        x1 = x_ref[...]
        for i, y_ref in enumerate((y0_ref, y1_ref, y2_ref)):
            x1 = x1 + _dot(y_ref[...], wo_ref[pl.ds(i * SEG, SEG), :])
        xn_ref[...] = (_rms_rows(x1) * nw_ref[...]).astype(BF16)
        acc_ref[...] = x1

    xn = xn_ref[...]
    acc = acc_ref[...]
    for c0 in range(0, wu_ref.shape[1], MLP_SUB):
        h = jnp.maximum(_dot(xn, wu_ref[:, pl.ds(c0, MLP_SUB)]), 0.0)
        acc = acc + _dot((h * h).astype(BF16), wd_ref[pl.ds(c0, MLP_SUB), :])
    acc_ref[...] = acc

    @pl.when(j == pl.num_programs(1) - 1)
    def _():
        out = acc_ref[...]
        if final_norm:
            out = _rms_rows(out) * fw_ref[...]
        o_ref[...] = out


def _mlp(x, ys, w_out, norm_w, w_up, w_down, final_w, final_norm, tm=1024, tf=2048):
    t, d = x.shape
    ff = w_up.shape[1]
    tm = min(tm, t)
    rows = lambda width: pl.BlockSpec((tm, width), lambda i, j: (i, 0))
    full = lambda a: pl.BlockSpec(a.shape, lambda i, j: (0,) * a.ndim)
    return pl.pallas_call(
        functools.partial(_mlp_kernel, final_norm=final_norm),
        grid=(t // tm, ff // tf),
        in_specs=[rows(d), rows(SEG), rows(SEG), rows(SEG), full(w_out), full(norm_w),
                  pl.BlockSpec((d, tf), lambda i, j: (0, j)),
                  pl.BlockSpec((tf, d), lambda i, j: (j, 0)),
                  full(final_w)],
        out_specs=rows(d),
        out_shape=jax.ShapeDtypeStruct((t, d), F32),
        scratch_shapes=[pltpu.VMEM((tm, d), BF16), pltpu.VMEM((tm, d), F32)],
        compiler_params=_params("parallel", "arbitrary"),
        name="mlp",
    )(x, *ys, w_out, norm_w, w_up, w_down, final_w)


def _split_w_in(w):
    sizes = (SEG, SEG, SEG, SEG, SEG, 2 * SEG, SSD_HEADS, 3 * SEG, SEG, GDN_HEADS, GDN_HEADS)
    offs = np.concatenate([[0], np.cumsum(sizes)])
    piece = lambda i: w[:, offs[i]:offs[i + 1]]
    small = jnp.concatenate([piece(6), piece(9), piece(10)], axis=1)
    small = jnp.pad(small, ((0, 0), (0, LANE - small.shape[1])))
    slab = lambda idx: jnp.concatenate([piece(i) for i in idx], axis=1)
    w_ret = slab((0, 1, 2, 3))
    w_ssd = jnp.concatenate([slab((4, 5)), small], axis=1)
    w_gdn = jnp.concatenate([slab((7, 8)), small], axis=1)
    return w_ret.astype(BF16), w_ssd.astype(BF16), w_gdn.astype(BF16)


def kernel(x, positions, mix_norm_w, w_in, ret_norm_w, ssd_conv_w, ssd_conv_b, ssd_dt_bias, ssd_a_log, ssd_d,
           ssd_norm_w, gdn_conv_w, gdn_dt_bias, gdn_a_log, gdn_norm_w, w_out, mlp_norm_w, w_up, w_down,
           final_norm_w):
    batch, s, d = x.shape
    depth = w_in.shape[0]
    xf = x.reshape(batch * s, d)
    cosf, sinf = _rope_tables(positions)
    for l in range(depth):
        w_ret, w_ssd, w_gdn = _split_w_in(w_in[l])
        mnw = mix_norm_w[l][None, :]
        y_ret = _retention(xf, mnw, w_ret, cosf, sinf, ret_norm_w[l][None, :], batch)
        y_ssd = _ssd(xf, mnw, w_ssd, ssd_conv_w[l], ssd_conv_b[l], ssd_dt_bias[l], ssd_a_log[l], ssd_d[l],
                     ssd_norm_w[l][None, :], batch)
        y_gdn = _gdn(xf, mnw, w_gdn, gdn_conv_w[l], gdn_dt_bias[l], gdn_a_log[l], gdn_norm_w[l], batch)
        xf = _mlp(xf, (y_ret, y_ssd, y_gdn), w_out[l].astype(BF16), mlp_norm_w[l][None, :], w_up[l].astype(BF16),
                  w_down[l].astype(BF16), final_norm_w[None, :], final_norm=(l == depth - 1))
    return xf.reshape(batch, s, d)
```

```python
import functools

import numpy as np
import jax
import jax.numpy as jnp
from jax import lax
from jax.experimental import pallas as pl
from jax.experimental.pallas import tpu as pltpu

F32 = jnp.float32
BF16 = jnp.bfloat16

NORM_EPS = 1e-6
L2_EPS = 1e-6
ROPE_BASE = 10000.0

RET_HEADS = 4
RET_CHUNK = 128
SSD_HEADS = 8
SSD_GROUPS = 2
SSD_CHUNK = 128
GDN_HEADS = 4
GDN_CHUNK = 64
GDN_BASE = 8
CONV_K = 4

LANE = 128
SEG = 512
CONV_PAD = 8
MLP_SUB = 1024
VMEM_LIMIT = 60000 * 1024

GDN_BETA_LANE = SSD_HEADS
GDN_DECAY_LANE = SSD_HEADS + GDN_HEADS


def _dot(a, b):
    return jnp.dot(a, b, preferred_element_type=F32)


def _dot_nt(a, b):
    return lax.dot_general(a, b, (((1,), (1,)), ((), ())), preferred_element_type=F32)


def _dot_tn(a, b):
    return lax.dot_general(a, b, (((0,), (0,)), ((), ())), preferred_element_type=F32)


def _split3(x):
    hi = x.astype(BF16)
    r = x - hi.astype(F32)
    mid = r.astype(BF16)
    lo = (r - mid.astype(F32)).astype(BF16)
    return hi, mid, lo


def _cumsum_pair(tri3, tri3_t, x):
    parts = jnp.concatenate(_split3(x), axis=0)
    return _dot(tri3, parts), _dot_tn(parts, tri3_t)


def _expand_heads(x, e3_ref, nheads):
    hi, mid, lo = _split3(x)
    lane = lax.broadcasted_iota(jnp.int32, x.shape, 1)
    packed = jnp.where(lane < nheads, hi.astype(F32),
                       jnp.where(lane < 2 * nheads, pltpu.roll(mid.astype(F32), nheads, 1),
                                 jnp.where(lane < 3 * nheads, pltpu.roll(lo.astype(F32), 2 * nheads, 1), 0.0)))
    return _dot(packed.astype(BF16), e3_ref[...])


def _silu(t):
    return t * jax.nn.sigmoid(t)


def _softplus(t):
    return jnp.maximum(t, 0.0) + jnp.log1p(jnp.exp(-jnp.abs(t)))


def _rms_rows(t, eps=NORM_EPS):
    return t * lax.rsqrt(jnp.mean(t * t, axis=-1, keepdims=True) + eps)


def _params(*sem):
    return pltpu.CompilerParams(dimension_semantics=sem, vmem_limit_bytes=VMEM_LIMIT)


def _rope_kernel(pos_ref, freq_ref, cos_ref, sin_ref):
    ang = pos_ref[...].astype(F32) * freq_ref[...]
    lane = lax.broadcasted_iota(jnp.int32, ang.shape, 1)
    cos_ref[...] = jnp.cos(ang)
    sin_ref[...] = jnp.where(lane < LANE // 2, -jnp.sin(ang), jnp.sin(ang))


def _rope_tables(positions, tb=1024):
    t = positions.size
    half = LANE // 2
    inv_freq = ROPE_BASE ** (-jnp.arange(half, dtype=F32) / half)
    freq = jnp.concatenate([inv_freq, inv_freq])[None, :]
    tb = min(tb, t)
    return pl.pallas_call(
        _rope_kernel,
        grid=(t // tb,),
        in_specs=[pl.BlockSpec((tb, 1), lambda i: (i, 0)), pl.BlockSpec((1, LANE), lambda i: (0, 0))],
        out_specs=[pl.BlockSpec((tb, LANE), lambda i: (i, 0))] * 2,
        out_shape=[jax.ShapeDtypeStruct((t, LANE), F32)] * 2,
        compiler_params=_params("parallel"),
        name="rope_tables",
    )(positions.reshape(t, 1), freq)


def _project(x_ref, nw_ref, w_ref, proj_ref):
    xn = (_rms_rows(x_ref[...]) * nw_ref[...]).astype(BF16)
    n = w_ref.shape[1]
    for c0 in range(0, n, SEG):
        width = min(SEG, n - c0)
        proj_ref[:, pl.ds(c0, width)] = _dot(xn, w_ref[:, pl.ds(c0, width)])


def _project_slabs(x_ref, nw_ref, w_ref, proj_ref):
    xn = (_rms_rows(x_ref[...]) * nw_ref[...]).astype(BF16)
    n = w_ref.shape[1]

    def slab(c0, width):
        proj_ref[:, pl.ds(c0, width)] = _dot(xn, w_ref[:, pl.ds(c0, width)])

    return [functools.partial(slab, c0, min(SEG, n - c0)) for c0 in range(0, n, SEG)]


def _project_first(x0_ref, nw_ref, w_ref, proj_ref):
    @pl.when((pl.program_id(0) == 0) & (pl.program_id(1) == 0))
    def _():
        _project(x0_ref, nw_ref, w_ref, proj_ref)


def _seg_view(proj_ref, j, width=SEG):
    return proj_ref.at[:, pl.ds(j * SEG, width)]


def _ret_consts():
    c = RET_CHUNK
    h = RET_HEADS
    log_gamma = jnp.log1p(-jnp.exp2(-5.0 - jnp.arange(h, dtype=F32)))
    idx = jnp.arange(c, dtype=F32)
    rel = idx[:, None] - idx[None, :]
    causal = rel >= 0
    d_intra = jnp.where(causal, jnp.exp(log_gamma[:, None, None] * jnp.where(causal, rel, 0.0)), 0.0)
    zeta = jnp.exp(log_gamma[:, None] * (c - 1 - idx))
    xi = jnp.exp(log_gamma[:, None] * (idx + 1))
    cdec = jnp.exp(log_gamma * c)
    ones = jnp.ones((h, c, LANE), F32)
    tab = jnp.stack([zeta[:, :, None] * ones, xi[:, :, None] * ones, cdec[:, None, None] * ones], axis=1)
    return d_intra, tab


def _ret_kernel(x_ref, mnw_ref, w_ref, cos_ref, sin_ref, dm_ref, tab_ref, nw_ref, o_ref, proj_ref, st_ref, *,
                nchunk):
    c = RET_CHUNK

    @pl.when(pl.program_id(1) == 0)
    def _():
        st_ref[...] = jnp.zeros_like(st_ref)

    _project(x_ref, mnw_ref, w_ref, proj_ref)
    q_ref, k_ref, v_ref, g_ref = (_seg_view(proj_ref, j) for j in range(4))

    scale = LANE ** -0.5
    work = []
    for ci in range(nchunk):
        rows = pl.ds(ci * c, c)
        cos = cos_ref[rows, :]
        sin = sin_ref[rows, :]
        for h in range(RET_HEADS):
            cols = pl.ds(h * LANE, LANE)
            q = q_ref[rows, cols]
            k = k_ref[rows, cols]
            vb = v_ref[rows, cols].astype(BF16)
            q = (q * cos + pltpu.roll(q, LANE // 2, 1) * sin) * scale
            k = k * cos + pltpu.roll(k, LANE // 2, 1) * sin
            qb = q.astype(BF16)
            scores = _dot_nt(qb, k.astype(BF16)) * dm_ref[h]
            work.append(dict(qb=qb, y=_dot(scores.astype(BF16), vb),
                             kv=_dot_tn((k * tab_ref[h, 0]).astype(BF16), vb)))
    states = [st_ref[h] for h in range(RET_HEADS)]
    for ci in range(nchunk):
        rows = pl.ds(ci * c, c)
        for h in range(RET_HEADS):
            cols = pl.ds(h * LANE, LANE)
            wk = work[ci * RET_HEADS + h]
            y = wk["y"] + _dot(wk["qb"], states[h].astype(BF16)) * tab_ref[h, 1]
            states[h] = tab_ref[h, 2] * states[h] + wk["kv"]
            o_ref[rows, cols] = (_rms_rows(y) * nw_ref[:, cols] * _silu(g_ref[rows, cols])).astype(BF16)
    for h in range(RET_HEADS):
        st_ref[h] = states[h]


def _mixer_call(body, name, x, mix_norm_w, w, row_inputs, consts, scratch, batch, tb, pipelined):
    t, d = x.shape
    nb = t // batch // tb
    last = t // tb - 1
    rows = lambda width: pl.BlockSpec((tb, width), lambda b, i: (b * nb + i, 0))
    full = lambda a: pl.BlockSpec(a.shape, lambda b, i: (0,) * a.ndim)
    if pipelined:
        x_specs = [pl.BlockSpec((tb, d), lambda b, i: (0, 0)),
                   pl.BlockSpec((tb, d), lambda b, i: (jnp.minimum(b * nb + i + 1, last), 0))]
    else:
        x_specs = [rows(d)]
    in_specs = x_specs + [full(mix_norm_w), full(w)]
    in_specs += [rows(a.shape[1]) for a in row_inputs] + [full(a) for a in consts]
    consts = list(row_inputs) + list(consts)
    return pl.pallas_call(
        body,
        grid=(batch, nb),
        in_specs=in_specs,
        out_specs=rows(SEG),
        out_shape=jax.ShapeDtypeStruct((t, SEG), BF16),
        scratch_shapes=[pltpu.VMEM((tb, w.shape[1]), F32)] + scratch,
        compiler_params=_params("arbitrary", "arbitrary"),
        name=name,
    )(*([x] * len(x_specs)), mix_norm_w, w, *consts)


def _retention(x, mix_norm_w, w, cosf, sinf, norm_w, batch, tb=512):
    tb = min(tb, x.shape[0] // batch)
    d_intra, tab = _ret_consts()
    return _mixer_call(functools.partial(_ret_kernel, nchunk=tb // RET_CHUNK), "retention", x, mix_norm_w, w,
                       [cosf, sinf], [d_intra, tab, norm_w], [pltpu.VMEM((RET_HEADS, LANE, LANE), F32)], batch, tb,
                       pipelined=False)


def _causal_conv(x_ref, ext_ref, w_ref, first):
    tb = x_ref.shape[0]

    @pl.when(first)
    def _():
        ext_ref[pl.ds(0, CONV_PAD), :] = jnp.zeros((CONV_PAD, ext_ref.shape[1]), F32)

    x = x_ref[...]
    ext_ref[pl.ds(CONV_PAD, tb), :] = x
    acc = x * w_ref[CONV_K - 1:CONV_K, :]
    for j in range(CONV_K - 1):
        shift = CONV_K - 1 - j
        acc = acc + ext_ref[pl.ds(CONV_PAD - shift, tb), :] * w_ref[j:j + 1, :]
    ext_ref[pl.ds(0, CONV_PAD), :] = x[tb - CONV_PAD:, :]
    return acc


def _tri(n, strict=False, upper=False):
    r = lax.broadcasted_iota(jnp.int32, (n, n), 0)
    c = lax.broadcasted_iota(jnp.int32, (n, n), 1)
    if upper:
        r, c = c, r
    return (r > c) if strict else (r >= c)


def _ssd_kernel(x0_ref, xnext_ref, mnw_ref, w_ref, cwx_ref, cwbc_ref, cbx_ref, cbbc_ref, hp_ref, dfull_ref, e_ref,
                nw_ref, o_ref, proj_ref, extx_ref, extbc_ref, st_ref, *, nchunk):
    c = SSD_CHUNK
    nh = SSD_HEADS
    first = pl.program_id(1) == 0

    @pl.when(first)
    def _():
        st_ref[...] = jnp.zeros_like(st_ref)

    _project_first(x0_ref, mnw_ref, w_ref, proj_ref)
    z_ref, x_ref, bc_ref = (_seg_view(proj_ref, j) for j in range(3))
    sm_ref = _seg_view(proj_ref, 3, LANE)

    slabs = _project_slabs(xnext_ref, mnw_ref, w_ref, proj_ref)
    sm_all = sm_ref[...]
    xs_all = _silu(_causal_conv(x_ref, extx_ref, cwx_ref, first) + cbx_ref[...])
    slabs[1]()
    bc_all = _silu(_causal_conv(bc_ref, extbc_ref, cwbc_ref, first) + cbbc_ref[...])
    slabs[2]()
    gate_all = _silu(z_ref[...])
    slabs[0]()
    slabs[3]()
    dt_bias = hp_ref[0:1, :]
    a_neg = -jnp.exp(hp_ref[1:2, :])
    d_full = dfull_ref[...]
    causal = _tri(c)
    tri3 = jnp.concatenate([causal.astype(BF16)] * 3, axis=1)
    tri3_t = jnp.concatenate([_tri(c, upper=True).astype(BF16)] * 3, axis=0)
    lane = lax.broadcasted_iota(jnp.int32, (c, LANE), 1)
    gw = SEG // SSD_GROUPS
    ks = LANE
    states = [st_ref[g] for g in range(SSD_GROUPS)]

    for ci in range(nchunk):
        r0 = ci * c
        xs = xs_all[r0:r0 + c, :]
        bc = bc_all[r0:r0 + c, :]
        dt = _softplus(sm_all[r0:r0 + c, :] + dt_bias)
        acs_col, acs_row = _cumsum_pair(tri3, tri3_t, dt * a_neg)
        a_last = acs_col[c - 1:c, :]
        dt_full = _expand_heads(dt, e_ref, nh)
        ea_full = _expand_heads(jnp.exp(acs_col), e_ref, nh)
        te_full = _expand_heads(jnp.exp(a_last - acs_col), e_ref, nh)
        cd_full = ea_full[c - 1:c, :]
        xdt = xs * dt_full
        y_parts = []
        for g in range(SSD_GROUPS):
            bm = bc[:, g * ks:(g + 1) * ks].astype(BF16)
            cm = bc[:, SSD_GROUPS * ks + g * ks:SSD_GROUPS * ks + (g + 1) * ks].astype(BF16)
            cb = _dot_nt(cm, bm)
            gcols = slice(g * gw, (g + 1) * gw)
            prev = states[g]
            y_inter = _dot(cm, prev.astype(BF16)) * ea_full[:, gcols]
            states[g] = cd_full[:, gcols] * prev + _dot_tn(bm, (xdt[:, gcols] * te_full[:, gcols]).astype(BF16))
            for pair in range(gw // LANE):
                ms = []
                for sub in range(2):
                    hd = g * (nh // SSD_GROUPS) + pair * 2 + sub
                    seg = acs_col[:, hd:hd + 1] - acs_row[hd:hd + 1, :]
                    dec = jnp.where(causal, jnp.exp(jnp.where(causal, seg, 0.0)), 0.0)
                    ms.append((cb * dec).astype(BF16))
                xp = xdt[:, g * gw + pair * LANE:g * gw + (pair + 1) * LANE]
                lo = jnp.where(lane < LANE // 2, xp, 0.0).astype(BF16)
                hi = jnp.where(lane >= LANE // 2, xp, 0.0).astype(BF16)
                y_pair = _dot(jnp.concatenate(ms, axis=1), jnp.concatenate([lo, hi], axis=0))
                y_parts.append(y_pair + y_inter[:, pair * LANE:(pair + 1) * LANE])
        y = jnp.concatenate(y_parts, axis=1) + xs * d_full
        y = y * gate_all[r0:r0 + c, :]
        outs = [_rms_rows(y[:, g * gw:(g + 1) * gw]) for g in range(SSD_GROUPS)]
        o_ref[pl.ds(r0, c), :] = (jnp.concatenate(outs, axis=1) * nw_ref[...]).astype(BF16)
    for g in range(SSD_GROUPS):
        st_ref[g] = states[g]


def _pad_rows(a, rows=8, cols=LANE):
    out = jnp.zeros((rows, cols), F32)
    return out.at[:a.shape[0], :a.shape[1]].set(a.astype(F32))


def _ssd(x, mix_norm_w, w, conv_w, conv_b, dt_bias, a_log, d_skip, norm_w, batch, tb=512):
    tb = min(tb, x.shape[0] // batch)
    hp = _pad_rows(jnp.stack([dt_bias, a_log]))
    hw = SEG // SSD_HEADS
    d_full = jnp.repeat(d_skip.astype(F32), hw)[None, :]
    expand = jnp.tile(jnp.repeat(jnp.eye(SSD_HEADS, dtype=BF16), hw, axis=1), (3, 1))
    expand = jnp.pad(expand, ((0, LANE - 3 * SSD_HEADS), (0, 0)))
    cwx, cwbc = conv_w[:, :SEG], conv_w[:, SEG:]
    cbx, cbbc = conv_b[None, :SEG], conv_b[None, SEG:]
    consts = [cwx, cwbc, cbx, cbbc, hp, d_full, expand, norm_w]
    scratch = [pltpu.VMEM((tb + CONV_PAD, SEG), F32), pltpu.VMEM((tb + CONV_PAD, SEG), F32),
               pltpu.VMEM((SSD_GROUPS, LANE, SEG // SSD_GROUPS), F32)]
    return _mixer_call(functools.partial(_ssd_kernel, nchunk=tb // SSD_CHUNK), "ssd", x, mix_norm_w, w,
                       [], consts, scratch, batch, tb, pipelined=True)


def _gdn_kernel(x0_ref, xnext_ref, mnw_ref, w_ref, cwq_ref, cwk_ref, cwv_ref, hp_ref, nw_ref,
                o_ref, proj_ref, extq_ref, extk_ref, extv_ref, st_ref, *, nchunk):
    c = GDN_CHUNK
    nh = GDN_HEADS
    first = pl.program_id(1) == 0

    @pl.when(first)
    def _():
        st_ref[...] = jnp.zeros_like(st_ref)

    _project_first(x0_ref, mnw_ref, w_ref, proj_ref)
    q_ref, k_ref, v_ref, z_ref = (_seg_view(proj_ref, j) for j in range(4))
    sm_ref = _seg_view(proj_ref, 4, LANE)

    slabs = _project_slabs(xnext_ref, mnw_ref, w_ref, proj_ref)
    sm = sm_ref[...]
    q_all = _silu(_causal_conv(q_ref, extq_ref, cwq_ref, first))
    slabs[0]()
    k_all = _silu(_causal_conv(k_ref, extk_ref, cwk_ref, first))
    slabs[1]()
    v_all = _silu(_causal_conv(v_ref, extv_ref, cwv_ref, first))
    slabs[2]()
    gate_all = _silu(z_ref[...])
    slabs[3]()
    slabs[4]()
    dt_bias = hp_ref[0:1, :]
    a_neg = -jnp.exp(hp_ref[1:2, :])
    scale = LANE ** -0.5
    pw = nh * c
    beta = jax.nn.sigmoid(sm)
    g = a_neg * _softplus(sm + dt_bias)
    tri3 = jnp.concatenate([_tri(c).astype(BF16)] * 3, axis=1)
    tri3_t = jnp.concatenate([_tri(c, upper=True).astype(BF16)] * 3, axis=0)
    tri3_t2 = jnp.concatenate([tri3_t] * (LANE // c), axis=1)

    row_p = lax.broadcasted_iota(jnp.int32, (c, pw), 0)
    lane_p = lax.broadcasted_iota(jnp.int32, (c, pw), 1)
    col_p = jnp.bitwise_and(lane_p, c - 1)
    incl_p = row_p >= col_p
    strict_p = row_p > col_p
    eye_p = (row_p == col_p).astype(F32)
    same_p = lambda n: (row_p // n) == (col_p // n)
    lane_1 = lax.broadcasted_iota(jnp.int32, (c, LANE), 1)
    head_p = [((lane_p // c) == h).astype(BF16) for h in range(nh)]
    lane_s = lax.broadcasted_iota(jnp.int32, (c, SEG), 1)
    head_s = [((lane_s // LANE) == h).astype(BF16) for h in range(nh)]

    def block_rows(y, masks):
        return jnp.concatenate([y * m for m in masks], axis=0)

    def bcast_nat(x, lane0):
        r = x.shape[0]
        return jnp.concatenate([jnp.broadcast_to(x[:, lane0 + h:lane0 + h + 1], (r, LANE)) for h in range(nh)], axis=1)

    def bcast_packed(x, lane0):
        tiles = []
        for t in range(pw // LANE):
            per = LANE // c
            tile = jnp.broadcast_to(x[:, lane0 + t * per:lane0 + t * per + 1], (c, LANE))
            for j in range(1, per):
                tile = jnp.where(lane_1 < j * c, tile,
                                 jnp.broadcast_to(x[:, lane0 + t * per + j:lane0 + t * per + j + 1], (c, LANE)))
            tiles.append(tile)
        return jnp.concatenate(tiles, axis=1)

    def l2n(t):
        parts = []
        for h in range(nh):
            th = t[:, h * LANE:(h + 1) * LANE]
            parts.append(th * lax.rsqrt(jnp.sum(th * th, axis=-1, keepdims=True) + L2_EPS))
        return jnp.concatenate(parts, axis=1)

    qn_all = l2n(q_all) * scale
    kn_all = l2n(k_all)

    work = []
    for ci in range(nchunk):
        r0 = ci * c
        gcs_col, gcs_row = _cumsum_pair(tri3, tri3_t2, g[r0:r0 + c, :])
        g_last = gcs_col[c - 1:c, :]
        eg = bcast_nat(jnp.exp(gcs_col), GDN_DECAY_LANE)
        ekd = bcast_nat(jnp.exp(g_last - gcs_col), GDN_DECAY_LANE)
        cdec = bcast_nat(jnp.exp(g_last), GDN_DECAY_LANE)
        bfull = bcast_nat(beta[r0:r0 + c, :], GDN_BETA_LANE)
        q = qn_all[r0:r0 + c, :]
        k = kn_all[r0:r0 + c, :]
        kb = k * bfull
        per = LANE // c
        row_tiles = []
        for t in range(pw // LANE):
            tile = gcs_row[GDN_DECAY_LANE + t * per:GDN_DECAY_LANE + t * per + 1, :]
            for j in range(1, per):
                tile = jnp.where(lane_1[0:1] < j * c, tile,
                                 gcs_row[GDN_DECAY_LANE + t * per + j:GDN_DECAY_LANE + t * per + j + 1, :])
            row_tiles.append(tile)
        diff = bcast_packed(gcs_col, GDN_DECAY_LANE) - jnp.concatenate(row_tiles, axis=1)
        decay = jnp.where(incl_p, jnp.exp(jnp.where(incl_p, diff, 0.0)), 0.0)
        kq = _dot_nt(jnp.concatenate([kb, q], axis=0).astype(BF16), block_rows(k.astype(BF16), head_s))
        work.append(dict(
            low=jnp.where(strict_p, kq[:c] * decay, 0.0),
            attn=(kq[c:] * decay).astype(BF16),
            vb=(v_all[r0:r0 + c, :] * bfull).astype(BF16),
            kbe=(kb * eg).astype(BF16),
            qd=(q * eg).astype(BF16),
            kd=(k * ekd).astype(BF16),
            cdec=cdec))

    pmul = lambda a, b: _dot(a, block_rows(b, head_p))
    for wk in work:
        wk["n"] = jnp.where(same_p(GDN_BASE), -wk["low"], 0.0).astype(BF16)
    for wk in work:
        wk["n2"] = pmul(wk["n"], wk["n"]).astype(BF16)
        wk["t"] = eye_p + wk["n"].astype(F32)
    for wk in work:
        wk["t"] = wk["t"] + pmul(wk["t"].astype(BF16), wk["n2"])
        wk["n4"] = pmul(wk["n2"], wk["n2"]).astype(BF16)
    for wk in work:
        wk["t"] = wk["t"] + pmul(wk["t"].astype(BF16), wk["n4"])
    n = GDN_BASE
    while n < c:
        off = same_p(2 * n) & jnp.logical_not(same_p(n))
        for wk in work:
            tb16 = wk["t"].astype(BF16)
            wk["tc"] = (pmul(tb16, jnp.where(off, wk["low"], 0.0).astype(BF16)).astype(BF16), tb16)
        for wk in work:
            tc, tb16 = wk["tc"]
            wk["t"] = wk["t"] - pmul(tc, tb16)
        n *= 2
    for wk in work:
        tb16 = wk["t"].astype(BF16)
        wk["u"] = _dot(tb16, block_rows(wk["vb"], head_s))
        wk["w"] = _dot(tb16, block_rows(wk["kbe"], head_s))

    states = [st_ref[h] for h in range(nh)]
    for ci in range(nchunk):
        rows = pl.ds(ci * c, c)
        wk = work[ci]
        ws_parts = []
        for h in range(nh):
            hc = slice(h * LANE, (h + 1) * LANE)
            lhs = jnp.concatenate([wk["w"][:, hc].astype(BF16), wk["qd"][:, hc]], axis=0)
            ws_parts.append(_dot(lhs, states[h].astype(BF16)))
        ws = jnp.concatenate(ws_parts, axis=1)
        v_new = (wk["u"] - ws[:c]).astype(BF16)
        o = ws[c:] + _dot(wk["attn"], block_rows(v_new, head_s))
        for h in range(nh):
            hc = slice(h * LANE, (h + 1) * LANE)
            states[h] = states[h] * wk["cdec"][:, hc] + _dot_tn(wk["kd"][:, hc], v_new[:, hc])
        outs = [_rms_rows(o[:, h * LANE:(h + 1) * LANE]) * nw_ref[...] for h in range(nh)]
        o_ref[rows, :] = (jnp.concatenate(outs, axis=1) * gate_all[ci * c:(ci + 1) * c, :]).astype(BF16)
    for h in range(nh):
        st_ref[h] = states[h]


def _gdn(x, mix_norm_w, w, conv_w, dt_bias, a_log, norm_w, batch, tb=512):
    tb = min(tb, x.shape[0] // batch)
    hp = jnp.zeros((8, LANE), F32).at[0:2, GDN_DECAY_LANE:GDN_DECAY_LANE + GDN_HEADS].set(
        jnp.stack([dt_bias, a_log]).astype(F32))
    cws = [conv_w[:, i * SEG:(i + 1) * SEG] for i in range(3)]
    scratch = [pltpu.VMEM((tb + CONV_PAD, SEG), F32)] * 3 + [pltpu.VMEM((GDN_HEADS, LANE, LANE), F32)]
    return _mixer_call(functools.partial(_gdn_kernel, nchunk=tb // GDN_CHUNK), "gdn", x, mix_norm_w, w,
                       [], cws + [hp, norm_w[None, :]], scratch, batch, tb, pipelined=True)


def _mlp_kernel(x_ref, y0_ref, y1_ref, y2_ref, wo_ref, nw_ref, wu_ref, wd_ref, fw_ref, o_ref, xn_ref, acc_ref, *,
                final_norm):
    j = pl.program_id(1)

    @pl.when(j == 0)
    def _():
        x1 = x_ref[...]
        for i, y_ref in enumerate((y0_ref, y1_ref, y2_ref)):
            x1 = x1 + _dot(y_ref[...], wo_ref[pl.ds(i * SEG, SEG), :])
        xn_ref[...] = (_rms_rows(x1) * nw_ref[...]).astype(BF16)
        acc_ref[...] = x1

    xn = xn_ref[...]
    acc = acc_ref[...]
    for c0 in range(0, wu_ref.shape[1], MLP_SUB):
        h = jnp.maximum(_dot(xn, wu_ref[:, pl.ds(c0, MLP_SUB)]), 0.0)
        acc = acc + _dot((h * h).astype(BF16), wd_ref[pl.ds(c0, MLP_SUB), :])
    acc_ref[...] = acc

    @pl.when(j == pl.num_programs(1) - 1)
    def _():
        out = acc_ref[...]
        if final_norm:
            out = _rms_rows(out) * fw_ref[...]
        o_ref[...] = out


def _mlp(x, ys, w_out, norm_w, w_up, w_down, final_w, final_norm, tm=1024, tf=2048):
    t, d = x.shape
    ff = w_up.shape[1]
    tm = min(tm, t)
    rows = lambda width: pl.BlockSpec((tm, width), lambda i, j: (i, 0))
    full = lambda a: pl.BlockSpec(a.shape, lambda i, j: (0,) * a.ndim)
    return pl.pallas_call(
        functools.partial(_mlp_kernel, final_norm=final_norm),
        grid=(t // tm, ff // tf),
        in_specs=[rows(d), rows(SEG), rows(SEG), rows(SEG), full(w_out), full(norm_w),
                  pl.BlockSpec((d, tf), lambda i, j: (0, j)),
                  pl.BlockSpec((tf, d), lambda i, j: (j, 0)),
                  full(final_w)],
        out_specs=rows(d),
        out_shape=jax.ShapeDtypeStruct((t, d), F32),
        scratch_shapes=[pltpu.VMEM((tm, d), BF16), pltpu.VMEM((tm, d), F32)],
        compiler_params=_params("parallel", "arbitrary"),
        name="mlp",
    )(x, *ys, w_out, norm_w, w_up, w_down, final_w)


def _split_w_in(w):
    sizes = (SEG, SEG, SEG, SEG, SEG, 2 * SEG, SSD_HEADS, 3 * SEG, SEG, GDN_HEADS, GDN_HEADS)
    offs = np.concatenate([[0], np.cumsum(sizes)])
    piece = lambda i: w[:, offs[i]:offs[i + 1]]
    small = jnp.concatenate([piece(6), piece(9), piece(10)], axis=1)
    small = jnp.pad(small, ((0, 0), (0, LANE - small.shape[1])))
    slab = lambda idx: jnp.concatenate([piece(i) for i in idx], axis=1)
    w_ret = slab((0, 1, 2, 3))
    w_ssd = jnp.concatenate([slab((4, 5)), small], axis=1)
    w_gdn = jnp.concatenate([slab((7, 8)), small], axis=1)
    return w_ret.astype(BF16), w_ssd.astype(BF16), w_gdn.astype(BF16)


def kernel(x, positions, mix_norm_w, w_in, ret_norm_w, ssd_conv_w, ssd_conv_b, ssd_dt_bias, ssd_a_log, ssd_d,
           ssd_norm_w, gdn_conv_w, gdn_dt_bias, gdn_a_log, gdn_norm_w, w_out, mlp_norm_w, w_up, w_down,
           final_norm_w):
    batch, s, d = x.shape
    depth = w_in.shape[0]
    xf = x.reshape(batch * s, d)
    cosf, sinf = _rope_tables(positions)
    for l in range(depth):
        w_ret, w_ssd, w_gdn = _split_w_in(w_in[l])
        mnw = mix_norm_w[l][None, :]
        y_ret = _retention(xf, mnw, w_ret, cosf, sinf, ret_norm_w[l][None, :], batch)
        y_ssd = _ssd(xf, mnw, w_ssd, ssd_conv_w[l], ssd_conv_b[l], ssd_dt_bias[l], ssd_a_log[l], ssd_d[l],
                     ssd_norm_w[l][None, :], batch)
        y_gdn = _gdn(xf, mnw, w_gdn, gdn_conv_w[l], gdn_dt_bias[l], gdn_a_log[l], gdn_norm_w[l], batch)
        xf = _mlp(xf, (y_ret, y_ssd, y_gdn), w_out[l].astype(BF16), mlp_norm_w[l][None, :], w_up[l].astype(BF16),
                  w_down[l].astype(BF16), final_norm_w[None, :], final_norm=(l == depth - 1))
    return xf.reshape(batch, s, d)
```

```python
import functools

import numpy as np
import jax
import jax.numpy as jnp
from jax import lax
from jax.experimental import pallas as pl
from jax.experimental.pallas import tpu as pltpu

F32 = jnp.float32
BF16 = jnp.bfloat16

NORM_EPS = 1e-6
L2_EPS = 1e-6
ROPE_BASE = 10000.0

RET_HEADS = 4
RET_CHUNK = 128
SSD_HEADS = 8
SSD_GROUPS = 2
SSD_CHUNK = 128
GDN_HEADS = 4
GDN_CHUNK = 64
GDN_BASE = 8
CONV_K = 4

LANE = 128
SEG = 512
CONV_PAD = 8
MLP_SUB = 1024
VMEM_LIMIT = 60000 * 1024

GDN_BETA_LANE = SSD_HEADS
GDN_DECAY_LANE = SSD_HEADS + GDN_HEADS


def _dot(a, b):
    return jnp.dot(a, b, preferred_element_type=F32)


def _dot_nt(a, b):
    return lax.dot_general(a, b, (((1,), (1,)), ((), ())), preferred_element_type=F32)


def _dot_tn(a, b):
    return lax.dot_general(a, b, (((0,), (0,)), ((), ())), preferred_element_type=F32)


def _split3(x):
    hi = x.astype(BF16)
    r = x - hi.astype(F32)
    mid = r.astype(BF16)
    lo = (r - mid.astype(F32)).astype(BF16)
    return hi, mid, lo


def _cumsum_pair(tri3, tri3_t, x):
    parts = jnp.concatenate(_split3(x), axis=0)
    return _dot(tri3, parts), _dot_tn(parts, tri3_t)


def _expand_heads(x, e3_ref, nheads):
    hi, mid, lo = _split3(x)
    lane = lax.broadcasted_iota(jnp.int32, x.shape, 1)
    packed = jnp.where(lane < nheads, hi.astype(F32),
                       jnp.where(lane < 2 * nheads, pltpu.roll(mid.astype(F32), nheads, 1),
                                 jnp.where(lane < 3 * nheads, pltpu.roll(lo.astype(F32), 2 * nheads, 1), 0.0)))
    return _dot(packed.astype(BF16), e3_ref[...])


def _silu(t):
    return t * jax.nn.sigmoid(t)


def _softplus(t):
    return jnp.maximum(t, 0.0) + jnp.log1p(jnp.exp(-jnp.abs(t)))


def _rms_rows(t, eps=NORM_EPS):
    return t * lax.rsqrt(jnp.mean(t * t, axis=-1, keepdims=True) + eps)


def _params(*sem):
    return pltpu.CompilerParams(dimension_semantics=sem, vmem_limit_bytes=VMEM_LIMIT)


def _rope_kernel(pos_ref, freq_ref, cos_ref, sin_ref):
    ang = pos_ref[...].astype(F32) * freq_ref[...]
    lane = lax.broadcasted_iota(jnp.int32, ang.shape, 1)
    cos_ref[...] = jnp.cos(ang)
    sin_ref[...] = jnp.where(lane < LANE // 2, -jnp.sin(ang), jnp.sin(ang))


def _rope_tables(positions, tb=1024):
    t = positions.size
    half = LANE // 2
    inv_freq = ROPE_BASE ** (-jnp.arange(half, dtype=F32) / half)
    freq = jnp.concatenate([inv_freq, inv_freq])[None, :]
    tb = min(tb, t)
    return pl.pallas_call(
        _rope_kernel,
        grid=(t // tb,),
        in_specs=[pl.BlockSpec((tb, 1), lambda i: (i, 0)), pl.BlockSpec((1, LANE), lambda i: (0, 0))],
        out_specs=[pl.BlockSpec((tb, LANE), lambda i: (i, 0))] * 2,
        out_shape=[jax.ShapeDtypeStruct((t, LANE), F32)] * 2,
        compiler_params=_params("parallel"),
        name="rope_tables",
    )(positions.reshape(t, 1), freq)


def _normed(x_ref, nw_ref):
    if x_ref.dtype == BF16:
        return x_ref[...]
    return (_rms_rows(x_ref[...]) * nw_ref[...]).astype(BF16)


def _project(x_ref, nw_ref, w_ref, proj_ref, xn_out_ref=None):
    xn = _normed(x_ref, nw_ref)
    if xn_out_ref is not None:
        xn_out_ref[...] = xn
    n = w_ref.shape[1]
    for c0 in range(0, n, SEG):
        width = min(SEG, n - c0)
        proj_ref[:, pl.ds(c0, width)] = _dot(xn, w_ref[:, pl.ds(c0, width)])


def _project_slabs(x_ref, nw_ref, w_ref, proj_ref):
    xn = _normed(x_ref, nw_ref)
    n = w_ref.shape[1]

    def slab(c0, width):
        proj_ref[:, pl.ds(c0, width)] = _dot(xn, w_ref[:, pl.ds(c0, width)])

    return [functools.partial(slab, c0, min(SEG, n - c0)) for c0 in range(0, n, SEG)]


def _project_first(x0_ref, nw_ref, w_ref, proj_ref):
    @pl.when((pl.program_id(0) == 0) & (pl.program_id(1) == 0))
    def _():
        _project(x0_ref, nw_ref, w_ref, proj_ref)


def _seg_view(proj_ref, j, width=SEG):
    return proj_ref.at[:, pl.ds(j * SEG, width)]


def _ret_consts():
    c = RET_CHUNK
    h = RET_HEADS
    log_gamma = jnp.log1p(-jnp.exp2(-5.0 - jnp.arange(h, dtype=F32)))
    idx = jnp.arange(c, dtype=F32)
    rel = idx[:, None] - idx[None, :]
    causal = rel >= 0
    d_intra = jnp.where(causal, jnp.exp(log_gamma[:, None, None] * jnp.where(causal, rel, 0.0)), 0.0)
    zeta = jnp.exp(log_gamma[:, None] * (c - 1 - idx))
    xi = jnp.exp(log_gamma[:, None] * (idx + 1))
    cdec = jnp.exp(log_gamma * c)
    ones = jnp.ones((h, c, LANE), F32)
    tab = jnp.stack([zeta[:, :, None] * ones, xi[:, :, None] * ones, cdec[:, None, None] * ones], axis=1)
    return d_intra, tab


def _ret_kernel(x_ref, mnw_ref, w_ref, cos_ref, sin_ref, dm_ref, tab_ref, nw_ref, o_ref, xn_ref, proj_ref, st_ref, *,
                nchunk):
    c = RET_CHUNK

    @pl.when(pl.program_id(1) == 0)
    def _():
        st_ref[...] = jnp.zeros_like(st_ref)

    _project(x_ref, mnw_ref, w_ref, proj_ref, xn_ref)
    q_ref, k_ref, v_ref, g_ref = (_seg_view(proj_ref, j) for j in range(4))

    scale = LANE ** -0.5
    work = []
    for ci in range(nchunk):
        rows = pl.ds(ci * c, c)
        cos = cos_ref[rows, :]
        sin = sin_ref[rows, :]
        for h in range(RET_HEADS):
            cols = pl.ds(h * LANE, LANE)
            q = q_ref[rows, cols]
            k = k_ref[rows, cols]
            vb = v_ref[rows, cols].astype(BF16)
            q = (q * cos + pltpu.roll(q, LANE // 2, 1) * sin) * scale
            k = k * cos + pltpu.roll(k, LANE // 2, 1) * sin
            qb = q.astype(BF16)
            scores = _dot_nt(qb, k.astype(BF16)) * dm_ref[h]
            work.append(dict(qb=qb, y=_dot(scores.astype(BF16), vb),
                             kv=_dot_tn((k * tab_ref[h, 0]).astype(BF16), vb)))
    states = [st_ref[h] for h in range(RET_HEADS)]
    for ci in range(nchunk):
        rows = pl.ds(ci * c, c)
        for h in range(RET_HEADS):
            cols = pl.ds(h * LANE, LANE)
            wk = work[ci * RET_HEADS + h]
            y = wk["y"] + _dot(wk["qb"], states[h].astype(BF16)) * tab_ref[h, 1]
            states[h] = tab_ref[h, 2] * states[h] + wk["kv"]
            o_ref[rows, cols] = (_rms_rows(y) * nw_ref[:, cols] * _silu(g_ref[rows, cols])).astype(BF16)
    for h in range(RET_HEADS):
        st_ref[h] = states[h]


def _mixer_call(body, name, x, mix_norm_w, w, row_inputs, consts, scratch, batch, tb, pipelined, emit_xn=False):
    t, d = x.shape
    nb = t // batch // tb
    last = t // tb - 1
    rows = lambda width: pl.BlockSpec((tb, width), lambda b, i: (b * nb + i, 0))
    full = lambda a: pl.BlockSpec(a.shape, lambda b, i: (0,) * a.ndim)
    if pipelined:
        x_specs = [pl.BlockSpec((tb, d), lambda b, i: (0, 0)),
                   pl.BlockSpec((tb, d), lambda b, i: (jnp.minimum(b * nb + i + 1, last), 0))]
    else:
        x_specs = [rows(d)]
    in_specs = x_specs + [full(mix_norm_w), full(w)]
    in_specs += [rows(a.shape[1]) for a in row_inputs] + [full(a) for a in consts]
    consts = list(row_inputs) + list(consts)
    return pl.pallas_call(
        body,
        grid=(batch, nb),
        in_specs=in_specs,
        out_specs=[rows(SEG), rows(d)] if emit_xn else rows(SEG),
        out_shape=([jax.ShapeDtypeStruct((t, SEG), BF16), jax.ShapeDtypeStruct((t, d), BF16)] if emit_xn
                   else jax.ShapeDtypeStruct((t, SEG), BF16)),
        scratch_shapes=[pltpu.VMEM((tb, w.shape[1]), F32)] + scratch,
        compiler_params=_params("arbitrary", "arbitrary"),
        name=name,
    )(*([x] * len(x_specs)), mix_norm_w, w, *consts)


def _retention(x, mix_norm_w, w, cosf, sinf, norm_w, batch, tb=512):
    tb = min(tb, x.shape[0] // batch)
    d_intra, tab = _ret_consts()
    return _mixer_call(functools.partial(_ret_kernel, nchunk=tb // RET_CHUNK), "retention", x, mix_norm_w, w,
                       [cosf, sinf], [d_intra, tab, norm_w], [pltpu.VMEM((RET_HEADS, LANE, LANE), F32)], batch, tb,
                       pipelined=False, emit_xn=True)


def _causal_conv(x_ref, ext_ref, w_ref, first):
    tb = x_ref.shape[0]

    @pl.when(first)
    def _():
        ext_ref[pl.ds(0, CONV_PAD), :] = jnp.zeros((CONV_PAD, ext_ref.shape[1]), F32)

    x = x_ref[...]
    ext_ref[pl.ds(CONV_PAD, tb), :] = x
    acc = x * w_ref[CONV_K - 1:CONV_K, :]
    for j in range(CONV_K - 1):
        shift = CONV_K - 1 - j
        acc = acc + ext_ref[pl.ds(CONV_PAD - shift, tb), :] * w_ref[j:j + 1, :]
    ext_ref[pl.ds(0, CONV_PAD), :] = x[tb - CONV_PAD:, :]
    return acc


def _tri(n, strict=False, upper=False):
    r = lax.broadcasted_iota(jnp.int32, (n, n), 0)
    c = lax.broadcasted_iota(jnp.int32, (n, n), 1)
    if upper:
        r, c = c, r
    return (r > c) if strict else (r >= c)


def _ssd_kernel(x0_ref, xnext_ref, mnw_ref, w_ref, cwx_ref, cwbc_ref, cbx_ref, cbbc_ref, hp_ref, dfull_ref, e_ref,
                nw_ref, o_ref, proj_ref, extx_ref, extbc_ref, st_ref, *, nchunk):
    c = SSD_CHUNK
    nh = SSD_HEADS
    first = pl.program_id(1) == 0

    @pl.when(first)
    def _():
        st_ref[...] = jnp.zeros_like(st_ref)

    _project_first(x0_ref, mnw_ref, w_ref, proj_ref)
    z_ref, x_ref, bc_ref = (_seg_view(proj_ref, j) for j in range(3))
    sm_ref = _seg_view(proj_ref, 3, LANE)

    slabs = _project_slabs(xnext_ref, mnw_ref, w_ref, proj_ref)
    sm_all = sm_ref[...]
    xs_all = _silu(_causal_conv(x_ref, extx_ref, cwx_ref, first) + cbx_ref[...])
    slabs[1]()
    bc_all = _silu(_causal_conv(bc_ref, extbc_ref, cwbc_ref, first) + cbbc_ref[...])
    slabs[2]()
    gate_all = _silu(z_ref[...])
    slabs[0]()
    slabs[3]()
    dt_bias = hp_ref[0:1, :]
    a_neg = -jnp.exp(hp_ref[1:2, :])
    d_full = dfull_ref[...]
    causal = _tri(c)
    tri3 = jnp.concatenate([causal.astype(BF16)] * 3, axis=1)
    tri3_t = jnp.concatenate([_tri(c, upper=True).astype(BF16)] * 3, axis=0)
    lane = lax.broadcasted_iota(jnp.int32, (c, LANE), 1)
    gw = SEG // SSD_GROUPS
    ks = LANE
    states = [st_ref[g] for g in range(SSD_GROUPS)]

    for ci in range(nchunk):
        r0 = ci * c
        xs = xs_all[r0:r0 + c, :]
        bc = bc_all[r0:r0 + c, :]
        dt = _softplus(sm_all[r0:r0 + c, :] + dt_bias)
        acs_col, acs_row = _cumsum_pair(tri3, tri3_t, dt * a_neg)
        a_last = acs_col[c - 1:c, :]
        dt_full = _expand_heads(dt, e_ref, nh)
        ea_full = _expand_heads(jnp.exp(acs_col), e_ref, nh)
        te_full = _expand_heads(jnp.exp(a_last - acs_col), e_ref, nh)
        cd_full = ea_full[c - 1:c, :]
        xdt = xs * dt_full
        y_parts = []
        for g in range(SSD_GROUPS):
            bm = bc[:, g * ks:(g + 1) * ks].astype(BF16)
            cm = bc[:, SSD_GROUPS * ks + g * ks:SSD_GROUPS * ks + (g + 1) * ks].astype(BF16)
            cb = _dot_nt(cm, bm)
            gcols = slice(g * gw, (g + 1) * gw)
            prev = states[g]
            y_inter = _dot(cm, prev.astype(BF16)) * ea_full[:, gcols]
            states[g] = cd_full[:, gcols] * prev + _dot_tn(bm, (xdt[:, gcols] * te_full[:, gcols]).astype(BF16))
            for pair in range(gw // LANE):
                ms = []
                for sub in range(2):
                    hd = g * (nh // SSD_GROUPS) + pair * 2 + sub
                    seg = acs_col[:, hd:hd + 1] - acs_row[hd:hd + 1, :]
                    dec = jnp.where(causal, jnp.exp(jnp.where(causal, seg, 0.0)), 0.0)
                    ms.append((cb * dec).astype(BF16))
                xp = xdt[:, g * gw + pair * LANE:g * gw + (pair + 1) * LANE]
                lo = jnp.where(lane < LANE // 2, xp, 0.0).astype(BF16)
                hi = jnp.where(lane >= LANE // 2, xp, 0.0).astype(BF16)
                y_pair = _dot(jnp.concatenate(ms, axis=1), jnp.concatenate([lo, hi], axis=0))
                y_parts.append(y_pair + y_inter[:, pair * LANE:(pair + 1) * LANE])
        y = jnp.concatenate(y_parts, axis=1) + xs * d_full
        y = y * gate_all[r0:r0 + c, :]
        outs = [_rms_rows(y[:, g * gw:(g + 1) * gw]) for g in range(SSD_GROUPS)]
        o_ref[pl.ds(r0, c), :] = (jnp.concatenate(outs, axis=1) * nw_ref[...]).astype(BF16)
    for g in range(SSD_GROUPS):
        st_ref[g] = states[g]


def _pad_rows(a, rows=8, cols=LANE):
    out = jnp.zeros((rows, cols), F32)
    return out.at[:a.shape[0], :a.shape[1]].set(a.astype(F32))


def _ssd(x, mix_norm_w, w, conv_w, conv_b, dt_bias, a_log, d_skip, norm_w, batch, tb=512):
    tb = min(tb, x.shape[0] // batch)
    hp = _pad_rows(jnp.stack([dt_bias, a_log]))
    hw = SEG // SSD_HEADS
    d_full = jnp.repeat(d_skip.astype(F32), hw)[None, :]
    expand = jnp.tile(jnp.repeat(jnp.eye(SSD_HEADS, dtype=BF16), hw, axis=1), (3, 1))
    expand = jnp.pad(expand, ((0, LANE - 3 * SSD_HEADS), (0, 0)))
    cwx, cwbc = conv_w[:, :SEG], conv_w[:, SEG:]
    cbx, cbbc = conv_b[None, :SEG], conv_b[None, SEG:]
    consts = [cwx, cwbc, cbx, cbbc, hp, d_full, expand, norm_w]
    scratch = [pltpu.VMEM((tb + CONV_PAD, SEG), F32), pltpu.VMEM((tb + CONV_PAD, SEG), F32),
               pltpu.VMEM((SSD_GROUPS, LANE, SEG // SSD_GROUPS), F32)]
    return _mixer_call(functools.partial(_ssd_kernel, nchunk=tb // SSD_CHUNK), "ssd", x, mix_norm_w, w,
                       [], consts, scratch, batch, tb, pipelined=True)


def _gdn_kernel(x0_ref, xnext_ref, mnw_ref, w_ref, cwq_ref, cwk_ref, cwv_ref, hp_ref, nw_ref,
                o_ref, proj_ref, extq_ref, extk_ref, extv_ref, st_ref, *, nchunk):
    c = GDN_CHUNK
    nh = GDN_HEADS
    first = pl.program_id(1) == 0

    @pl.when(first)
    def _():
        st_ref[...] = jnp.zeros_like(st_ref)

    _project_first(x0_ref, mnw_ref, w_ref, proj_ref)
    q_ref, k_ref, v_ref, z_ref = (_seg_view(proj_ref, j) for j in range(4))
    sm_ref = _seg_view(proj_ref, 4, LANE)

    slabs = _project_slabs(xnext_ref, mnw_ref, w_ref, proj_ref)
    sm = sm_ref[...]
    q_all = _silu(_causal_conv(q_ref, extq_ref, cwq_ref, first))
    slabs[0]()
    k_all = _silu(_causal_conv(k_ref, extk_ref, cwk_ref, first))
    slabs[1]()
    v_all = _silu(_causal_conv(v_ref, extv_ref, cwv_ref, first))
    slabs[2]()
    gate_all = _silu(z_ref[...])
    slabs[3]()
    slabs[4]()
    dt_bias = hp_ref[0:1, :]
    a_neg = -jnp.exp(hp_ref[1:2, :])
    scale = LANE ** -0.5
    pw = nh * c
    beta = jax.nn.sigmoid(sm)
    g = a_neg * _softplus(sm + dt_bias)
    tri3 = jnp.concatenate([_tri(c).astype(BF16)] * 3, axis=1)
    tri3_t = jnp.concatenate([_tri(c, upper=True).astype(BF16)] * 3, axis=0)
    tri3_t2 = jnp.concatenate([tri3_t] * (LANE // c), axis=1)

    row_p = lax.broadcasted_iota(jnp.int32, (c, pw), 0)
    lane_p = lax.broadcasted_iota(jnp.int32, (c, pw), 1)
    col_p = jnp.bitwise_and(lane_p, c - 1)
    incl_p = row_p >= col_p
    strict_p = row_p > col_p
    eye_p = (row_p == col_p).astype(F32)
    same_p = lambda n: (row_p // n) == (col_p // n)
    lane_1 = lax.broadcasted_iota(jnp.int32, (c, LANE), 1)
    head_p = [((lane_p // c) == h).astype(BF16) for h in range(nh)]
    lane_s = lax.broadcasted_iota(jnp.int32, (c, SEG), 1)
    head_s = [((lane_s // LANE) == h).astype(BF16) for h in range(nh)]

    def block_rows(y, masks):
        return jnp.concatenate([y * m for m in masks], axis=0)

    def bcast_nat(x, lane0):
        r = x.shape[0]
        return jnp.concatenate([jnp.broadcast_to(x[:, lane0 + h:lane0 + h + 1], (r, LANE)) for h in range(nh)], axis=1)

    def bcast_packed(x, lane0):
        tiles = []
        for t in range(pw // LANE):
            per = LANE // c
            tile = jnp.broadcast_to(x[:, lane0 + t * per:lane0 + t * per + 1], (c, LANE))
            for j in range(1, per):
                tile = jnp.where(lane_1 < j * c, tile,
                                 jnp.broadcast_to(x[:, lane0 + t * per + j:lane0 + t * per + j + 1], (c, LANE)))
            tiles.append(tile)
        return jnp.concatenate(tiles, axis=1)

    def l2n(t):
        parts = []
        for h in range(nh):
            th = t[:, h * LANE:(h + 1) * LANE]
            parts.append(th * lax.rsqrt(jnp.sum(th * th, axis=-1, keepdims=True) + L2_EPS))
        return jnp.concatenate(parts, axis=1)

    qn_all = l2n(q_all) * scale
    kn_all = l2n(k_all)

    work = []
    for ci in range(nchunk):
        r0 = ci * c
        gcs_col, gcs_row = _cumsum_pair(tri3, tri3_t2, g[r0:r0 + c, :])
        g_last = gcs_col[c - 1:c, :]
        eg = bcast_nat(jnp.exp(gcs_col), GDN_DECAY_LANE)
        ekd = bcast_nat(jnp.exp(g_last - gcs_col), GDN_DECAY_LANE)
        cdec = bcast_nat(jnp.exp(g_last), GDN_DECAY_LANE)
        bfull = bcast_nat(beta[r0:r0 + c, :], GDN_BETA_LANE)
        q = qn_all[r0:r0 + c, :]
        k = kn_all[r0:r0 + c, :]
        kb = k * bfull
        per = LANE // c
        row_tiles = []
        for t in range(pw // LANE):
            tile = gcs_row[GDN_DECAY_LANE + t * per:GDN_DECAY_LANE + t * per + 1, :]
            for j in range(1, per):
                tile = jnp.where(lane_1[0:1] < j * c, tile,
                                 gcs_row[GDN_DECAY_LANE + t * per + j:GDN_DECAY_LANE + t * per + j + 1, :])
            row_tiles.append(tile)
        diff = bcast_packed(gcs_col, GDN_DECAY_LANE) - jnp.concatenate(row_tiles, axis=1)
        decay = jnp.where(incl_p, jnp.exp(jnp.where(incl_p, diff, 0.0)), 0.0)
        kq = _dot_nt(jnp.concatenate([kb, q], axis=0).astype(BF16), block_rows(k.astype(BF16), head_s))
        work.append(dict(
            low=jnp.where(strict_p, kq[:c] * decay, 0.0),
            attn=(kq[c:] * decay).astype(BF16),
            vb=(v_all[r0:r0 + c, :] * bfull).astype(BF16),
            kbe=(kb * eg).astype(BF16),
            qd=(q * eg).astype(BF16),
            kd=(k * ekd).astype(BF16),
            cdec=cdec))

    pmul = lambda a, b: _dot(a, block_rows(b, head_p))
    for wk in work:
        wk["n"] = jnp.where(same_p(GDN_BASE), -wk["low"], 0.0).astype(BF16)
    for wk in work:
        wk["n2"] = pmul(wk["n"], wk["n"]).astype(BF16)
        wk["t"] = eye_p + wk["n"].astype(F32)
    for wk in work:
        wk["t"] = wk["t"] + pmul(wk["t"].astype(BF16), wk["n2"])
        wk["n4"] = pmul(wk["n2"], wk["n2"]).astype(BF16)
    for wk in work:
        wk["t"] = wk["t"] + pmul(wk["t"].astype(BF16), wk["n4"])
    n = GDN_BASE
    while n < c:
        off = same_p(2 * n) & jnp.logical_not(same_p(n))
        for wk in work:
            tb16 = wk["t"].astype(BF16)
            wk["tc"] = (pmul(tb16, jnp.where(off, wk["low"], 0.0).astype(BF16)).astype(BF16), tb16)
        for wk in work:
            tc, tb16 = wk["tc"]
            wk["t"] = wk["t"] - pmul(tc, tb16)
        n *= 2
    for wk in work:
        tb16 = wk["t"].astype(BF16)
        wk["u"] = _dot(tb16, block_rows(wk["vb"], head_s))
        wk["w"] = _dot(tb16, block_rows(wk["kbe"], head_s))

    states = [st_ref[h] for h in range(nh)]
    for ci in range(nchunk):
        rows = pl.ds(ci * c, c)
        wk = work[ci]
        ws_parts = []
        for h in range(nh):
            hc = slice(h * LANE, (h + 1) * LANE)
            lhs = jnp.concatenate([wk["w"][:, hc].astype(BF16), wk["qd"][:, hc]], axis=0)
            ws_parts.append(_dot(lhs, states[h].astype(BF16)))
        ws = jnp.concatenate(ws_parts, axis=1)
        v_new = (wk["u"] - ws[:c]).astype(BF16)
        o = ws[c:] + _dot(wk["attn"], block_rows(v_new, head_s))
        for h in range(nh):
            hc = slice(h * LANE, (h + 1) * LANE)
            states[h] = states[h] * wk["cdec"][:, hc] + _dot_tn(wk["kd"][:, hc], v_new[:, hc])
        outs = [_rms_rows(o[:, h * LANE:(h + 1) * LANE]) * nw_ref[...] for h in range(nh)]
        o_ref[rows, :] = (jnp.concatenate(outs, axis=1) * gate_all[ci * c:(ci + 1) * c, :]).astype(BF16)
    for h in range(nh):
        st_ref[h] = states[h]


def _gdn(x, mix_norm_w, w, conv_w, dt_bias, a_log, norm_w, batch, tb=512):
    tb = min(tb, x.shape[0] // batch)
    hp = jnp.zeros((8, LANE), F32).at[0:2, GDN_DECAY_LANE:GDN_DECAY_LANE + GDN_HEADS].set(
        jnp.stack([dt_bias, a_log]).astype(F32))
    cws = [conv_w[:, i * SEG:(i + 1) * SEG] for i in range(3)]
    scratch = [pltpu.VMEM((tb + CONV_PAD, SEG), F32)] * 3 + [pltpu.VMEM((GDN_HEADS, LANE, LANE), F32)]
    return _mixer_call(functools.partial(_gdn_kernel, nchunk=tb // GDN_CHUNK), "gdn", x, mix_norm_w, w,
                       [], cws + [hp, norm_w[None, :]], scratch, batch, tb, pipelined=True)


def _mlp_kernel(x_ref, y0_ref, y1_ref, y2_ref, wo_ref, nw_ref, wu_ref, wd_ref, fw_ref, o_ref, xn_ref, acc_ref, *,
                final_norm):
    j = pl.program_id(1)

    @pl.when(j == 0)
    def _():
        x1 = x_ref[...]
        for i, y_ref in enumerate((y0_ref, y1_ref, y2_ref)):
            x1 = x1 + _dot(y_ref[...], wo_ref[pl.ds(i * SEG, SEG), :])
        xn_ref[...] = (_rms_rows(x1) * nw_ref[...]).astype(BF16)
        acc_ref[...] = x1

    xn = xn_ref[...]
    acc = acc_ref[...]
    for c0 in range(0, wu_ref.shape[1], MLP_SUB):
        h = jnp.maximum(_dot(xn, wu_ref[:, pl.ds(c0, MLP_SUB)]), 0.0)
        acc = acc + _dot((h * h).astype(BF16), wd_ref[pl.ds(c0, MLP_SUB), :])
    acc_ref[...] = acc

    @pl.when(j == pl.num_programs(1) - 1)
    def _():
        out = acc_ref[...]
        if final_norm:
            out = _rms_rows(out) * fw_ref[...]
        o_ref[...] = out


def _mlp(x, ys, w_out, norm_w, w_up, w_down, final_w, final_norm, tm=1024, tf=2048):
    t, d = x.shape
    ff = w_up.shape[1]
    tm = min(tm, t)
    rows = lambda width: pl.BlockSpec((tm, width), lambda i, j: (i, 0))
    full = lambda a: pl.BlockSpec(a.shape, lambda i, j: (0,) * a.ndim)
    return pl.pallas_call(
        functools.partial(_mlp_kernel, final_norm=final_norm),
        grid=(t // tm, ff // tf),
        in_specs=[rows(d), rows(SEG), rows(SEG), rows(SEG), full(w_out), full(norm_w),
                  pl.BlockSpec((d, tf), lambda i, j: (0, j)),
                  pl.BlockSpec((tf, d), lambda i, j: (j, 0)),
                  full(final_w)],
        out_specs=rows(d),
        out_shape=jax.ShapeDtypeStruct((t, d), F32),
        scratch_shapes=[pltpu.VMEM((tm, d), BF16), pltpu.VMEM((tm, d), F32)],
        compiler_params=_params("parallel", "arbitrary"),
        name="mlp",
    )(x, *ys, w_out, norm_w, w_up, w_down, final_w)


def _split_w_in(w):
    sizes = (SEG, SEG, SEG, SEG, SEG, 2 * SEG, SSD_HEADS, 3 * SEG, SEG, GDN_HEADS, GDN_HEADS)
    offs = np.concatenate([[0], np.cumsum(sizes)])
    piece = lambda i: w[:, offs[i]:offs[i + 1]]
    small = jnp.concatenate([piece(6), piece(9), piece(10)], axis=1)
    small = jnp.pad(small, ((0, 0), (0, LANE - small.shape[1])))
    slab = lambda idx: jnp.concatenate([piece(i) for i in idx], axis=1)
    w_ret = slab((0, 1, 2, 3))
    w_ssd = jnp.concatenate([slab((4, 5)), small], axis=1)
    w_gdn = jnp.concatenate([slab((7, 8)), small], axis=1)
    return w_ret.astype(BF16), w_ssd.astype(BF16), w_gdn.astype(BF16)


def kernel(x, positions, mix_norm_w, w_in, ret_norm_w, ssd_conv_w, ssd_conv_b, ssd_dt_bias, ssd_a_log, ssd_d,
           ssd_norm_w, gdn_conv_w, gdn_dt_bias, gdn_a_log, gdn_norm_w, w_out, mlp_norm_w, w_up, w_down,
           final_norm_w):
    batch, s, d = x.shape
    depth = w_in.shape[0]
    xf = x.reshape(batch * s, d)
    cosf, sinf = _rope_tables(positions)
    for l in range(depth):
        w_ret, w_ssd, w_gdn = _split_w_in(w_in[l])
        mnw = mix_norm_w[l][None, :]
        y_ret, xn = _retention(xf, mnw, w_ret, cosf, sinf, ret_norm_w[l][None, :], batch)
        y_ssd = _ssd(xn, mnw, w_ssd, ssd_conv_w[l], ssd_conv_b[l], ssd_dt_bias[l], ssd_a_log[l], ssd_d[l],
                     ssd_norm_w[l][None, :], batch)
        y_gdn = _gdn(xn, mnw, w_gdn, gdn_conv_w[l], gdn_dt_bias[l], gdn_a_log[l], gdn_norm_w[l], batch)
        xf = _mlp(xf, (y_ret, y_ssd, y_gdn), w_out[l].astype(BF16), mlp_norm_w[l][None, :], w_up[l].astype(BF16),
                  w_down[l].astype(BF16), final_norm_w[None, :], final_norm=(l == depth - 1))
    return xf.reshape(batch, s, d)
```

```python
import functools

import numpy as np
import jax
import jax.numpy as jnp
from jax import lax
from jax.experimental import pallas as pl
from jax.experimental.pallas import tpu as pltpu

F32 = jnp.float32
BF16 = jnp.bfloat16

NORM_EPS = 1e-6
L2_EPS = 1e-6
ROPE_BASE = 10000.0

RET_HEADS = 4
RET_CHUNK = 128
SSD_HEADS = 8
SSD_GROUPS = 2
SSD_CHUNK = 128
GDN_HEADS = 4
GDN_CHUNK = 64
GDN_BASE = 8
CONV_K = 4

LANE = 128
SEG = 512
CONV_PAD = 8
MLP_SUB = 1024
VMEM_LIMIT = 60000 * 1024

GDN_BETA_LANE = SSD_HEADS
GDN_DECAY_LANE = SSD_HEADS + GDN_HEADS


def _dot(a, b):
    return jnp.dot(a, b, preferred_element_type=F32)


def _dot_nt(a, b):
    return lax.dot_general(a, b, (((1,), (1,)), ((), ())), preferred_element_type=F32)


def _dot_tn(a, b):
    return lax.dot_general(a, b, (((0,), (0,)), ((), ())), preferred_element_type=F32)


def _split3(x):
    hi = x.astype(BF16)
    r = x - hi.astype(F32)
    mid = r.astype(BF16)
    lo = (r - mid.astype(F32)).astype(BF16)
    return hi, mid, lo


def _cumsum_pair(tri3, tri3_t, x):
    parts = jnp.concatenate(_split3(x), axis=0)
    return _dot(tri3, parts), _dot_tn(parts, tri3_t)


def _expand_heads(x, e3_ref, nheads):
    hi, mid, lo = _split3(x)
    lane = lax.broadcasted_iota(jnp.int32, x.shape, 1)
    packed = jnp.where(lane < nheads, hi.astype(F32),
                       jnp.where(lane < 2 * nheads, pltpu.roll(mid.astype(F32), nheads, 1),
                                 jnp.where(lane < 3 * nheads, pltpu.roll(lo.astype(F32), 2 * nheads, 1), 0.0)))
    return _dot(packed.astype(BF16), e3_ref[...])


def _silu(t):
    return t * jax.nn.sigmoid(t)


def _softplus(t):
    return jnp.maximum(t, 0.0) + jnp.log1p(jnp.exp(-jnp.abs(t)))


def _rms_rows(t, eps=NORM_EPS):
    return t * lax.rsqrt(jnp.mean(t * t, axis=-1, keepdims=True) + eps)


def _params(*sem):
    return pltpu.CompilerParams(dimension_semantics=sem, vmem_limit_bytes=VMEM_LIMIT)


def _rope_kernel(pos_ref, freq_ref, cos_ref, sin_ref):
    ang = pos_ref[...].astype(F32) * freq_ref[...]
    lane = lax.broadcasted_iota(jnp.int32, ang.shape, 1)
    cos_ref[...] = jnp.cos(ang)
    sin_ref[...] = jnp.where(lane < LANE // 2, -jnp.sin(ang), jnp.sin(ang))


def _rope_tables(positions, tb=1024):
    t = positions.size
    half = LANE // 2
    inv_freq = ROPE_BASE ** (-jnp.arange(half, dtype=F32) / half)
    freq = jnp.concatenate([inv_freq, inv_freq])[None, :]
    tb = min(tb, t)
    return pl.pallas_call(
        _rope_kernel,
        grid=(t // tb,),
        in_specs=[pl.BlockSpec((tb, 1), lambda i: (i, 0)), pl.BlockSpec((1, LANE), lambda i: (0, 0))],
        out_specs=[pl.BlockSpec((tb, LANE), lambda i: (i, 0))] * 2,
        out_shape=[jax.ShapeDtypeStruct((t, LANE), F32)] * 2,
        compiler_params=_params("parallel"),
        name="rope_tables",
    )(positions.reshape(t, 1), freq)


def _normed(x_ref, nw_ref):
    if x_ref.dtype == BF16:
        return x_ref[...]
    return (_rms_rows(x_ref[...]) * nw_ref[...]).astype(BF16)


def _project(x_ref, nw_ref, w_ref, proj_ref, xn_out_ref=None):
    xn = _normed(x_ref, nw_ref)
    if xn_out_ref is not None:
        xn_out_ref[...] = xn
    n = w_ref.shape[1]
    for c0 in range(0, n, SEG):
        width = min(SEG, n - c0)
        proj_ref[:, pl.ds(c0, width)] = _dot(xn, w_ref[:, pl.ds(c0, width)])


def _project_slabs(x_ref, nw_ref, w_ref, proj_ref):
    xn = _normed(x_ref, nw_ref)
    n = w_ref.shape[1]

    def slab(c0, width):
        proj_ref[:, pl.ds(c0, width)] = _dot(xn, w_ref[:, pl.ds(c0, width)])

    return [functools.partial(slab, c0, min(SEG, n - c0)) for c0 in range(0, n, SEG)]


def _project_first(x0_ref, nw_ref, w_ref, proj_ref):
    @pl.when((pl.program_id(0) == 0) & (pl.program_id(1) == 0))
    def _():
        _project(x0_ref, nw_ref, w_ref, proj_ref)


def _seg_view(proj_ref, j, width=SEG):
    return proj_ref.at[:, pl.ds(j * SEG, width)]


def _ret_consts():
    c = RET_CHUNK
    h = RET_HEADS
    log_gamma = jnp.log1p(-jnp.exp2(-5.0 - jnp.arange(h, dtype=F32)))
    idx = jnp.arange(c, dtype=F32)
    rel = idx[:, None] - idx[None, :]
    causal = rel >= 0
    d_intra = jnp.where(causal, jnp.exp(log_gamma[:, None, None] * jnp.where(causal, rel, 0.0)), 0.0)
    zeta = jnp.exp(log_gamma[:, None] * (c - 1 - idx))
    xi = jnp.exp(log_gamma[:, None] * (idx + 1))
    cdec = jnp.exp(log_gamma * c)
    ones = jnp.ones((h, c, LANE), F32)
    tab = jnp.stack([zeta[:, :, None] * ones, xi[:, :, None] * ones, cdec[:, None, None] * ones], axis=1)
    return d_intra, tab


def _ret_kernel(x_ref, mnw_ref, w_ref, cos_ref, sin_ref, dm_ref, tab_ref, nw_ref, o_ref, xn_ref, proj_ref, st_ref, *,
                nchunk):
    c = RET_CHUNK

    @pl.when(pl.program_id(1) == 0)
    def _():
        st_ref[...] = jnp.zeros_like(st_ref)

    _project(x_ref, mnw_ref, w_ref, proj_ref, xn_ref)
    q_ref, k_ref, v_ref, g_ref = (_seg_view(proj_ref, j) for j in range(4))

    scale = LANE ** -0.5
    work = []
    for ci in range(nchunk):
        rows = pl.ds(ci * c, c)
        cos = cos_ref[rows, :]
        sin = sin_ref[rows, :]
        for h in range(RET_HEADS):
            cols = pl.ds(h * LANE, LANE)
            q = q_ref[rows, cols]
            k = k_ref[rows, cols]
            vb = v_ref[rows, cols].astype(BF16)
            q = (q * cos + pltpu.roll(q, LANE // 2, 1) * sin) * scale
            k = k * cos + pltpu.roll(k, LANE // 2, 1) * sin
            qb = q.astype(BF16)
            scores = _dot_nt(qb, k.astype(BF16)) * dm_ref[h]
            work.append(dict(qb=qb, y=_dot(scores.astype(BF16), vb),
                             kv=_dot_tn((k * tab_ref[h, 0]).astype(BF16), vb)))
    states = [st_ref[h] for h in range(RET_HEADS)]
    for ci in range(nchunk):
        rows = pl.ds(ci * c, c)
        for h in range(RET_HEADS):
            cols = pl.ds(h * LANE, LANE)
            wk = work[ci * RET_HEADS + h]
            y = wk["y"] + _dot(wk["qb"], states[h].astype(BF16)) * tab_ref[h, 1]
            states[h] = tab_ref[h, 2] * states[h] + wk["kv"]
            o_ref[rows, cols] = (_rms_rows(y) * nw_ref[:, cols] * _silu(g_ref[rows, cols])).astype(BF16)
    for h in range(RET_HEADS):
        st_ref[h] = states[h]


def _mixer_call(body, name, x, mix_norm_w, w, row_inputs, consts, scratch, batch, tb, pipelined, emit_xn=False):
    t, d = x.shape
    nb = t // batch // tb
    last = t // tb - 1
    rows = lambda width: pl.BlockSpec((tb, width), lambda b, i: (b * nb + i, 0))
    full = lambda a: pl.BlockSpec(a.shape, lambda b, i: (0,) * a.ndim)
    if pipelined:
        x_specs = [pl.BlockSpec((tb, d), lambda b, i: (0, 0)),
                   pl.BlockSpec((tb, d), lambda b, i: (jnp.minimum(b * nb + i + 1, last), 0))]
    else:
        x_specs = [rows(d)]
    in_specs = x_specs + [full(mix_norm_w), full(w)]
    in_specs += [rows(a.shape[1]) for a in row_inputs] + [full(a) for a in consts]
    consts = list(row_inputs) + list(consts)
    return pl.pallas_call(
        body,
        grid=(batch, nb),
        in_specs=in_specs,
        out_specs=[rows(SEG), rows(d)] if emit_xn else rows(SEG),
        out_shape=([jax.ShapeDtypeStruct((t, SEG), BF16), jax.ShapeDtypeStruct((t, d), BF16)] if emit_xn
                   else jax.ShapeDtypeStruct((t, SEG), BF16)),
        scratch_shapes=[pltpu.VMEM((tb, w.shape[1]), F32)] + scratch,
        compiler_params=_params("arbitrary", "arbitrary"),
        name=name,
    )(*([x] * len(x_specs)), mix_norm_w, w, *consts)


def _retention(x, mix_norm_w, w, cosf, sinf, norm_w, batch, tb=512):
    tb = min(tb, x.shape[0] // batch)
    d_intra, tab = _ret_consts()
    return _mixer_call(functools.partial(_ret_kernel, nchunk=tb // RET_CHUNK), "retention", x, mix_norm_w, w,
                       [cosf, sinf], [d_intra, tab, norm_w], [pltpu.VMEM((RET_HEADS, LANE, LANE), F32)], batch, tb,
                       pipelined=False, emit_xn=True)


def _causal_conv(x_ref, ext_ref, w_ref, first):
    tb = x_ref.shape[0]

    @pl.when(first)
    def _():
        ext_ref[pl.ds(0, CONV_PAD), :] = jnp.zeros((CONV_PAD, ext_ref.shape[1]), F32)

    x = x_ref[...]
    ext_ref[pl.ds(CONV_PAD, tb), :] = x
    acc = x * w_ref[CONV_K - 1:CONV_K, :]
    for j in range(CONV_K - 1):
        shift = CONV_K - 1 - j
        acc = acc + ext_ref[pl.ds(CONV_PAD - shift, tb), :] * w_ref[j:j + 1, :]
    ext_ref[pl.ds(0, CONV_PAD), :] = x[tb - CONV_PAD:, :]
    return acc


def _tri(n, strict=False, upper=False):
    r = lax.broadcasted_iota(jnp.int32, (n, n), 0)
    c = lax.broadcasted_iota(jnp.int32, (n, n), 1)
    if upper:
        r, c = c, r
    return (r > c) if strict else (r >= c)


def _ssd_kernel(x0_ref, xnext_ref, mnw_ref, w_ref, cwx_ref, cwbc_ref, cbx_ref, cbbc_ref, hp_ref, dfull_ref, e_ref,
                nw_ref, o_ref, proj_ref, extx_ref, extbc_ref, st_ref, *, nchunk):
    c = SSD_CHUNK
    nh = SSD_HEADS
    first = pl.program_id(1) == 0

    @pl.when(first)
    def _():
        st_ref[...] = jnp.zeros_like(st_ref)

    _project_first(x0_ref, mnw_ref, w_ref, proj_ref)
    z_ref, x_ref, bc_ref = (_seg_view(proj_ref, j) for j in range(3))
    sm_ref = _seg_view(proj_ref, 3, LANE)

    slabs = _project_slabs(xnext_ref, mnw_ref, w_ref, proj_ref)
    sm_all = sm_ref[...]
    xs_all = _silu(_causal_conv(x_ref, extx_ref, cwx_ref, first) + cbx_ref[...])
    slabs[1]()
    bc_all = _silu(_causal_conv(bc_ref, extbc_ref, cwbc_ref, first) + cbbc_ref[...])
    slabs[2]()
    gate_all = _silu(z_ref[...])
    slabs[0]()
    slabs[3]()
    dt_bias = hp_ref[0:1, :]
    a_neg = -jnp.exp(hp_ref[1:2, :])
    d_full = dfull_ref[...]
    causal = _tri(c)
    tri3 = jnp.concatenate([causal.astype(BF16)] * 3, axis=1)
    tri3_t = jnp.concatenate([_tri(c, upper=True).astype(BF16)] * 3, axis=0)
    lane = lax.broadcasted_iota(jnp.int32, (c, LANE), 1)
    gw = SEG // SSD_GROUPS
    ks = LANE
    states = [st_ref[g] for g in range(SSD_GROUPS)]

    for ci in range(nchunk):
        r0 = ci * c
        xs = xs_all[r0:r0 + c, :]
        bc = bc_all[r0:r0 + c, :]
        dt = _softplus(sm_all[r0:r0 + c, :] + dt_bias)
        acs_col, acs_row = _cumsum_pair(tri3, tri3_t, dt * a_neg)
        a_last = acs_col[c - 1:c, :]
        dt_full = _expand_heads(dt, e_ref, nh)
        ea_full = _expand_heads(jnp.exp(acs_col), e_ref, nh)
        te_full = _expand_heads(jnp.exp(a_last - acs_col), e_ref, nh)
        cd_full = ea_full[c - 1:c, :]
        xdt = xs * dt_full
        y_parts = []
        for g in range(SSD_GROUPS):
            bm = bc[:, g * ks:(g + 1) * ks].astype(BF16)
            cm = bc[:, SSD_GROUPS * ks + g * ks:SSD_GROUPS * ks + (g + 1) * ks].astype(BF16)
            cb = _dot_nt(cm, bm)
            gcols = slice(g * gw, (g + 1) * gw)
            prev = states[g]
            y_inter = _dot(cm, prev.astype(BF16)) * ea_full[:, gcols]
            states[g] = cd_full[:, gcols] * prev + _dot_tn(bm, (xdt[:, gcols] * te_full[:, gcols]).astype(BF16))
            for pair in range(gw // LANE):
                ms = []
                for sub in range(2):
                    hd = g * (nh // SSD_GROUPS) + pair * 2 + sub
                    seg = acs_col[:, hd:hd + 1] - acs_row[hd:hd + 1, :]
                    dec = jnp.where(causal, jnp.exp(jnp.where(causal, seg, 0.0)), 0.0)
                    ms.append((cb * dec).astype(BF16))
                xp = xdt[:, g * gw + pair * LANE:g * gw + (pair + 1) * LANE]
                lo = jnp.where(lane < LANE // 2, xp, 0.0).astype(BF16)
                hi = jnp.where(lane >= LANE // 2, xp, 0.0).astype(BF16)
                y_pair = _dot(jnp.concatenate(ms, axis=1), jnp.concatenate([lo, hi], axis=0))
                y_parts.append(y_pair + y_inter[:, pair * LANE:(pair + 1) * LANE])
        y = jnp.concatenate(y_parts, axis=1) + xs * d_full
        y = y * gate_all[r0:r0 + c, :]
        outs = [_rms_rows(y[:, g * gw:(g + 1) * gw]) for g in range(SSD_GROUPS)]
        o_ref[pl.ds(r0, c), :] = (jnp.concatenate(outs, axis=1) * nw_ref[...]).astype(BF16)
    for g in range(SSD_GROUPS):
        st_ref[g] = states[g]


def _pad_rows(a, rows=8, cols=LANE):
    out = jnp.zeros((rows, cols), F32)
    return out.at[:a.shape[0], :a.shape[1]].set(a.astype(F32))


def _ssd(x, mix_norm_w, w, conv_w, conv_b, dt_bias, a_log, d_skip, norm_w, batch, tb=1024):
    tb = min(tb, x.shape[0] // batch)
    hp = _pad_rows(jnp.stack([dt_bias, a_log]))
    hw = SEG // SSD_HEADS
    d_full = jnp.repeat(d_skip.astype(F32), hw)[None, :]
    expand = jnp.tile(jnp.repeat(jnp.eye(SSD_HEADS, dtype=BF16), hw, axis=1), (3, 1))
    expand = jnp.pad(expand, ((0, LANE - 3 * SSD_HEADS), (0, 0)))
    cwx, cwbc = conv_w[:, :SEG], conv_w[:, SEG:]
    cbx, cbbc = conv_b[None, :SEG], conv_b[None, SEG:]
    consts = [cwx, cwbc, cbx, cbbc, hp, d_full, expand, norm_w]
    scratch = [pltpu.VMEM((tb + CONV_PAD, SEG), F32), pltpu.VMEM((tb + CONV_PAD, SEG), F32),
               pltpu.VMEM((SSD_GROUPS, LANE, SEG // SSD_GROUPS), F32)]
    return _mixer_call(functools.partial(_ssd_kernel, nchunk=tb // SSD_CHUNK), "ssd", x, mix_norm_w, w,
                       [], consts, scratch, batch, tb, pipelined=True)


def _gdn_kernel(x0_ref, xnext_ref, mnw_ref, w_ref, cwq_ref, cwk_ref, cwv_ref, hp_ref, nw_ref,
                o_ref, proj_ref, extq_ref, extk_ref, extv_ref, st_ref, *, nchunk):
    c = GDN_CHUNK
    nh = GDN_HEADS
    first = pl.program_id(1) == 0

    @pl.when(first)
    def _():
        st_ref[...] = jnp.zeros_like(st_ref)

    _project_first(x0_ref, mnw_ref, w_ref, proj_ref)
    q_ref, k_ref, v_ref, z_ref = (_seg_view(proj_ref, j) for j in range(4))
    sm_ref = _seg_view(proj_ref, 4, LANE)

    slabs = _project_slabs(xnext_ref, mnw_ref, w_ref, proj_ref)
    sm = sm_ref[...]
    q_all = _silu(_causal_conv(q_ref, extq_ref, cwq_ref, first))
    slabs[0]()
    k_all = _silu(_causal_conv(k_ref, extk_ref, cwk_ref, first))
    slabs[1]()
    v_all = _silu(_causal_conv(v_ref, extv_ref, cwv_ref, first))
    slabs[2]()
    gate_all = _silu(z_ref[...])
    slabs[3]()
    slabs[4]()
    dt_bias = hp_ref[0:1, :]
    a_neg = -jnp.exp(hp_ref[1:2, :])
    scale = LANE ** -0.5
    pw = nh * c
    beta = jax.nn.sigmoid(sm)
    g = a_neg * _softplus(sm + dt_bias)
    tri3 = jnp.concatenate([_tri(c).astype(BF16)] * 3, axis=1)
    tri3_t = jnp.concatenate([_tri(c, upper=True).astype(BF16)] * 3, axis=0)
    tri3_t2 = jnp.concatenate([tri3_t] * (LANE // c), axis=1)

    row_p = lax.broadcasted_iota(jnp.int32, (c, pw), 0)
    lane_p = lax.broadcasted_iota(jnp.int32, (c, pw), 1)
    col_p = jnp.bitwise_and(lane_p, c - 1)
    incl_p = row_p >= col_p
    strict_p = row_p > col_p
    eye_p = (row_p == col_p).astype(F32)
    same_p = lambda n: (row_p // n) == (col_p // n)
    lane_1 = lax.broadcasted_iota(jnp.int32, (c, LANE), 1)
    head_p = [((lane_p // c) == h).astype(BF16) for h in range(nh)]
    lane_s = lax.broadcasted_iota(jnp.int32, (c, SEG), 1)
    head_s = [((lane_s // LANE) == h).astype(BF16) for h in range(nh)]

    def block_rows(y, masks):
        return jnp.concatenate([y * m for m in masks], axis=0)

    def bcast_nat(x, lane0):
        r = x.shape[0]
        return jnp.concatenate([jnp.broadcast_to(x[:, lane0 + h:lane0 + h + 1], (r, LANE)) for h in range(nh)], axis=1)

    def bcast_packed(x, lane0):
        tiles = []
        for t in range(pw // LANE):
            per = LANE // c
            tile = jnp.broadcast_to(x[:, lane0 + t * per:lane0 + t * per + 1], (c, LANE))
            for j in range(1, per):
                tile = jnp.where(lane_1 < j * c, tile,
                                 jnp.broadcast_to(x[:, lane0 + t * per + j:lane0 + t * per + j + 1], (c, LANE)))
            tiles.append(tile)
        return jnp.concatenate(tiles, axis=1)

    def l2n(t):
        parts = []
        for h in range(nh):
            th = t[:, h * LANE:(h + 1) * LANE]
            parts.append(th * lax.rsqrt(jnp.sum(th * th, axis=-1, keepdims=True) + L2_EPS))
        return jnp.concatenate(parts, axis=1)

    qn_all = l2n(q_all) * scale
    kn_all = l2n(k_all)

    work = []
    for ci in range(nchunk):
        r0 = ci * c
        gcs_col, gcs_row = _cumsum_pair(tri3, tri3_t2, g[r0:r0 + c, :])
        g_last = gcs_col[c - 1:c, :]
        eg = bcast_nat(jnp.exp(gcs_col), GDN_DECAY_LANE)
        ekd = bcast_nat(jnp.exp(g_last - gcs_col), GDN_DECAY_LANE)
        cdec = bcast_nat(jnp.exp(g_last), GDN_DECAY_LANE)
        bfull = bcast_nat(beta[r0:r0 + c, :], GDN_BETA_LANE)
        q = qn_all[r0:r0 + c, :]
        k = kn_all[r0:r0 + c, :]
        kb = k * bfull
        per = LANE // c
        row_tiles = []
        for t in range(pw // LANE):
            tile = gcs_row[GDN_DECAY_LANE + t * per:GDN_DECAY_LANE + t * per + 1, :]
            for j in range(1, per):
                tile = jnp.where(lane_1[0:1] < j * c, tile,
                                 gcs_row[GDN_DECAY_LANE + t * per + j:GDN_DECAY_LANE + t * per + j + 1, :])
            row_tiles.append(tile)
        diff = bcast_packed(gcs_col, GDN_DECAY_LANE) - jnp.concatenate(row_tiles, axis=1)
        decay = jnp.where(incl_p, jnp.exp(jnp.where(incl_p, diff, 0.0)), 0.0)
        kq = _dot_nt(jnp.concatenate([kb, q], axis=0).astype(BF16), block_rows(k.astype(BF16), head_s))
        work.append(dict(
            low=jnp.where(strict_p, kq[:c] * decay, 0.0),
            attn=(kq[c:] * decay).astype(BF16),
            vb=(v_all[r0:r0 + c, :] * bfull).astype(BF16),
            kbe=(kb * eg).astype(BF16),
            qd=(q * eg).astype(BF16),
            kd=(k * ekd).astype(BF16),
            cdec=cdec))

    pmul = lambda a, b: _dot(a, block_rows(b, head_p))
    for wk in work:
        wk["n"] = jnp.where(same_p(GDN_BASE), -wk["low"], 0.0).astype(BF16)
    for wk in work:
        wk["n2"] = pmul(wk["n"], wk["n"]).astype(BF16)
        wk["t"] = eye_p + wk["n"].astype(F32)
    for wk in work:
        wk["t"] = wk["t"] + pmul(wk["t"].astype(BF16), wk["n2"])
        wk["n4"] = pmul(wk["n2"], wk["n2"]).astype(BF16)
    for wk in work:
        wk["t"] = wk["t"] + pmul(wk["t"].astype(BF16), wk["n4"])
    n = GDN_BASE
    while n < c:
        off = same_p(2 * n) & jnp.logical_not(same_p(n))
        for wk in work:
            tb16 = wk["t"].astype(BF16)
            wk["tc"] = (pmul(tb16, jnp.where(off, wk["low"], 0.0).astype(BF16)).astype(BF16), tb16)
        for wk in work:
            tc, tb16 = wk["tc"]
            wk["t"] = wk["t"] - pmul(tc, tb16)
        n *= 2
    for wk in work:
        tb16 = wk["t"].astype(BF16)
        wk["u"] = _dot(tb16, block_rows(wk["vb"], head_s))
        wk["w"] = _dot(tb16, block_rows(wk["kbe"], head_s))

    states = [st_ref[h] for h in range(nh)]
    for ci in range(nchunk):
        rows = pl.ds(ci * c, c)
        wk = work[ci]
        ws_parts = []
        for h in range(nh):
            hc = slice(h * LANE, (h + 1) * LANE)
            lhs = jnp.concatenate([wk["w"][:, hc].astype(BF16), wk["qd"][:, hc]], axis=0)
            ws_parts.append(_dot(lhs, states[h].astype(BF16)))
        ws = jnp.concatenate(ws_parts, axis=1)
        v_new = (wk["u"] - ws[:c]).astype(BF16)
        o = ws[c:] + _dot(wk["attn"], block_rows(v_new, head_s))
        for h in range(nh):
            hc = slice(h * LANE, (h + 1) * LANE)
            states[h] = states[h] * wk["cdec"][:, hc] + _dot_tn(wk["kd"][:, hc], v_new[:, hc])
        outs = [_rms_rows(o[:, h * LANE:(h + 1) * LANE]) * nw_ref[...] for h in range(nh)]
        o_ref[rows, :] = (jnp.concatenate(outs, axis=1) * gate_all[ci * c:(ci + 1) * c, :]).astype(BF16)
    for h in range(nh):
        st_ref[h] = states[h]


def _gdn(x, mix_norm_w, w, conv_w, dt_bias, a_log, norm_w, batch, tb=1024):
    tb = min(tb, x.shape[0] // batch)
    hp = jnp.zeros((8, LANE), F32).at[0:2, GDN_DECAY_LANE:GDN_DECAY_LANE + GDN_HEADS].set(
        jnp.stack([dt_bias, a_log]).astype(F32))
    cws = [conv_w[:, i * SEG:(i + 1) * SEG] for i in range(3)]
    scratch = [pltpu.VMEM((tb + CONV_PAD, SEG), F32)] * 3 + [pltpu.VMEM((GDN_HEADS, LANE, LANE), F32)]
    return _mixer_call(functools.partial(_gdn_kernel, nchunk=tb // GDN_CHUNK), "gdn", x, mix_norm_w, w,
                       [], cws + [hp, norm_w[None, :]], scratch, batch, tb, pipelined=True)


def _mlp_kernel(x_ref, y0_ref, y1_ref, y2_ref, wo_ref, nw_ref, wu_ref, wd_ref, fw_ref, o_ref, xn_ref, acc_ref, *,
                final_norm):
    j = pl.program_id(1)

    @pl.when(j == 0)
    def _():
        x1 = x_ref[...]
        for i, y_ref in enumerate((y0_ref, y1_ref, y2_ref)):
            x1 = x1 + _dot(y_ref[...], wo_ref[pl.ds(i * SEG, SEG), :])
        xn_ref[...] = (_rms_rows(x1) * nw_ref[...]).astype(BF16)
        acc_ref[...] = x1

    xn = xn_ref[...]
    acc = acc_ref[...]
    for c0 in range(0, wu_ref.shape[1], MLP_SUB):
        h = jnp.maximum(_dot(xn, wu_ref[:, pl.ds(c0, MLP_SUB)]), 0.0)
        acc = acc + _dot((h * h).astype(BF16), wd_ref[pl.ds(c0, MLP_SUB), :])
    acc_ref[...] = acc

    @pl.when(j == pl.num_programs(1) - 1)
    def _():
        out = acc_ref[...]
        if final_norm:
            out = _rms_rows(out) * fw_ref[...]
        o_ref[...] = out


def _mlp(x, ys, w_out, norm_w, w_up, w_down, final_w, final_norm, tm=1024, tf=2048):
    t, d = x.shape
    ff = w_up.shape[1]
    tm = min(tm, t)
    rows = lambda width: pl.BlockSpec((tm, width), lambda i, j: (i, 0))
    full = lambda a: pl.BlockSpec(a.shape, lambda i, j: (0,) * a.ndim)
    return pl.pallas_call(
        functools.partial(_mlp_kernel, final_norm=final_norm),
        grid=(t // tm, ff // tf),
        in_specs=[rows(d), rows(SEG), rows(SEG), rows(SEG), full(w_out), full(norm_w),
                  pl.BlockSpec((d, tf), lambda i, j: (0, j)),
                  pl.BlockSpec((tf, d), lambda i, j: (j, 0)),
                  full(final_w)],
        out_specs=rows(d),
        out_shape=jax.ShapeDtypeStruct((t, d), F32),
        scratch_shapes=[pltpu.VMEM((tm, d), BF16), pltpu.VMEM((tm, d), F32)],
        compiler_params=_params("parallel", "arbitrary"),
        name="mlp",
    )(x, *ys, w_out, norm_w, w_up, w_down, final_w)


def _split_w_in(w):
    sizes = (SEG, SEG, SEG, SEG, SEG, 2 * SEG, SSD_HEADS, 3 * SEG, SEG, GDN_HEADS, GDN_HEADS)
    offs = np.concatenate([[0], np.cumsum(sizes)])
    piece = lambda i: w[:, offs[i]:offs[i + 1]]
    small = jnp.concatenate([piece(6), piece(9), piece(10)], axis=1)
    small = jnp.pad(small, ((0, 0), (0, LANE - small.shape[1])))
    slab = lambda idx: jnp.concatenate([piece(i) for i in idx], axis=1)
    w_ret = slab((0, 1, 2, 3))
    w_ssd = jnp.concatenate([slab((4, 5)), small], axis=1)
    w_gdn = jnp.concatenate([slab((7, 8)), small], axis=1)
    return w_ret.astype(BF16), w_ssd.astype(BF16), w_gdn.astype(BF16)


def kernel(x, positions, mix_norm_w, w_in, ret_norm_w, ssd_conv_w, ssd_conv_b, ssd_dt_bias, ssd_a_log, ssd_d,
           ssd_norm_w, gdn_conv_w, gdn_dt_bias, gdn_a_log, gdn_norm_w, w_out, mlp_norm_w, w_up, w_down,
           final_norm_w):
    batch, s, d = x.shape
    depth = w_in.shape[0]
    xf = x.reshape(batch * s, d)
    cosf, sinf = _rope_tables(positions)
    for l in range(depth):
        w_ret, w_ssd, w_gdn = _split_w_in(w_in[l])
        mnw = mix_norm_w[l][None, :]
        y_ret, xn = _retention(xf, mnw, w_ret, cosf, sinf, ret_norm_w[l][None, :], batch)
        y_ssd = _ssd(xn, mnw, w_ssd, ssd_conv_w[l], ssd_conv_b[l], ssd_dt_bias[l], ssd_a_log[l], ssd_d[l],
                     ssd_norm_w[l][None, :], batch)
        y_gdn = _gdn(xn, mnw, w_gdn, gdn_conv_w[l], gdn_dt_bias[l], gdn_a_log[l], gdn_norm_w[l], batch)
        xf = _mlp(xf, (y_ret, y_ssd, y_gdn), w_out[l].astype(BF16), mlp_norm_w[l][None, :], w_up[l].astype(BF16),
                  w_down[l].astype(BF16), final_norm_w[None, :], final_norm=(l == depth - 1))
    return xf.reshape(batch, s, d)
```
